```python
import jax, jax.numpy as jnp
from jax import lax
import numpy as np

D_MODEL = 1024
BATCH = 2
SEQ = 16384
DEPTH = 2

N_MIXERS = 2
ALPHA = (2 * DEPTH) ** 0.25
BETA = (8 * DEPTH) ** -0.25
LN_EPS = 1e-5
D_FF = 2816
GM_WIDTH = 3 * D_MODEL
GM_CHUNK = 128
GM_GROUPS = 16
GM_GCH = GM_WIDTH // GM_GROUPS
HEAD_DIM = 64
N_HEADS = D_MODEL // HEAD_DIM
N_KV = 4
HPG = N_HEADS // N_KV
CMP_BLOCK = 32
CMP_STRIDE = 16
CMP_HIDDEN = 256
SLC_BLOCK = 64
SLC_TOP = 16
WINDOW = 512
Q_BLOCK = 128
ROPE_THETA = 10000.0
NEG = -1e30
NSA_COLS = N_HEADS * HEAD_DIM + 6 * N_KV * HEAD_DIM + 3 * N_HEADS

kernel_name = 'hybrid_gmlp_nsa_macaron_deepnorm'


def layer_norm(x, g, b):
    xf = x.astype(jnp.float32)
    mu = xf.mean(-1, keepdims=True)
    var = jnp.square(xf - mu).mean(-1, keepdims=True)
    return ((xf - mu) * lax.rsqrt(var + LN_EPS) * g.astype(jnp.float32) + b.astype(jnp.float32)).astype(x.dtype)


def swiglu(x, w_in, w_out):
    gate, up = jnp.split(x @ w_in, 2, axis=-1)
    return (jax.nn.silu(gate) * up) @ w_out


def rope(x, pos):
    half = x.shape[-1] // 2
    freq = ROPE_THETA ** (-jnp.arange(half, dtype=jnp.float32) / half)
    ang = pos.astype(jnp.float32)[:, None] * freq
    shape = (pos.shape[0],) + (1,) * (x.ndim - 3) + (half,)
    cos = jnp.cos(ang).reshape(shape)
    sin = jnp.sin(ang).reshape(shape)
    xf = x.astype(jnp.float32)
    x1, x2 = xf[..., :half], xf[..., half:]
    return jnp.concatenate([x1 * cos - x2 * sin, x2 * cos + x1 * sin], axis=-1).astype(x.dtype)


def masked_softmax(s, mask):
    p = jax.nn.softmax(jnp.where(mask, s, NEG), axis=-1)
    return p * jnp.any(mask, axis=-1, keepdims=True)


def chunked_gmlp(x, w_in, ln_g, ln_b, w_s, b_s, w_out):
    B_, S_, _ = x.shape
    u, v = jnp.split(jax.nn.gelu(x @ w_in), 2, axis=-1)
    v = layer_norm(v, ln_g, ln_b)
    v = v.reshape(B_, S_ // GM_CHUNK, GM_CHUNK, GM_GROUPS, GM_GCH)
    causal = jnp.tril(jnp.ones((GM_CHUNK, GM_CHUNK), dtype=bool))
    w = jnp.where(causal, w_s, 0)
    v = jnp.einsum('gts,bnsgc->bntgc', w, v) + b_s.T[:, :, None]
    return (u * v.reshape(B_, S_, GM_WIDTH)) @ w_out


def nsa(x, w_in, cmp_pe_k, cmp_w1_k, cmp_w2_k, cmp_pe_v, cmp_w1_v, cmp_w2_v, w_out):
    B_, S_, _ = x.shape
    dt = x.dtype
    kvw = N_KV * HEAD_DIM
    splits = np.cumsum([N_HEADS * HEAD_DIM] + [kvw] * 6).tolist()
    q, k_c, v_c, k_s, v_s, k_w, v_w, g = jnp.split(x @ w_in, splits, axis=-1)
    kv = lambda t: t.reshape(B_, S_, N_KV, HEAD_DIM)
    pos = jnp.arange(S_)
    q = rope(q.reshape(B_, S_, N_KV, HPG, HEAD_DIM), pos)
    k_s = rope(kv(k_s), pos)
    k_w = rope(kv(k_w), pos)
    v_s, v_w = kv(v_s), kv(v_w)
    gates = jax.nn.sigmoid(g.astype(jnp.float32)).reshape(B_, S_, N_KV, HPG, 3).astype(dt)

    n_cmp = S_ // CMP_STRIDE - 1

    def compress(t, pe, w1, w2):
        ch = kv(t).reshape(B_, S_ // CMP_STRIDE, CMP_STRIDE, N_KV, HEAD_DIM)
        blk = jnp.concatenate([ch[:, :-1], ch[:, 1:]], axis=2) + pe[:, None, :]
        h = jax.nn.gelu(jnp.einsum('bnlgd,ldh->bngh', blk, w1))
        return h @ w2

    cmp_end = CMP_STRIDE * jnp.arange(n_cmp) + CMP_BLOCK - 1
    k_cmp = rope(compress(k_c, cmp_pe_k, cmp_w1_k, cmp_w2_k), cmp_end)
    v_cmp = compress(v_c, cmp_pe_v, cmp_w1_v, cmp_w2_v)

    n_slc = S_ // SLC_BLOCK
    n_top = min(SLC_TOP, n_slc)
    r = SLC_BLOCK // CMP_STRIDE
    k_sb = k_s.reshape(B_, n_slc, SLC_BLOCK, N_KV, HEAD_DIM)
    v_sb = v_s.reshape(B_, n_slc, SLC_BLOCK, N_KV, HEAD_DIM)
    k_wp = jnp.pad(k_w, ((0, 0), (WINDOW, 0), (0, 0), (0, 0)))
    v_wp = jnp.pad(v_w, ((0, 0), (WINDOW, 0), (0, 0), (0, 0)))
    scale = HEAD_DIM ** -0.5
    bi = jnp.arange(B_)[:, None, None, None]
    gi = jnp.arange(N_KV)[None, None, :, None]
    slc_ids = jnp.arange(n_slc)[None, :]
    n_qb = S_ // Q_BLOCK

    def block(args):
        qb_idx, qb, gb = args
        t = qb_idx * Q_BLOCK + jnp.arange(Q_BLOCK)
        s = jnp.einsum('bqghd,bngd->bqghn', qb, k_cmp).astype(jnp.float32) * scale
        p_cmp = masked_softmax(s, (cmp_end[None, :] <= t[:, None])[None, :, None, None, :])
        o_cmp = jnp.einsum('bqghn,bngd->bqghd', p_cmp.astype(dt), v_cmp)
        imp = jnp.pad(p_cmp.sum(axis=3), ((0, 0), (0, 0), (0, 0), (1, 1)))
        imp = imp[..., :r * n_slc].reshape(B_, Q_BLOCK, N_KV, n_slc, r).sum(-1) + imp[..., r::r]
        cur = (t // SLC_BLOCK)[:, None]
        forced = (slc_ids == 0) | (slc_ids == cur) | (slc_ids == cur - 1)
        imp = jnp.where(forced[:, None, :], 1e9, jnp.where((slc_ids <= cur)[:, None, :], imp, -1e9))
        _, idx = lax.top_k(imp, n_top)
        k_sel = k_sb[bi, idx, :, gi, :]
        v_sel = v_sb[bi, idx, :, gi, :].reshape(B_, Q_BLOCK, N_KV, n_top * SLC_BLOCK, HEAD_DIM)
        key_pos = idx[..., None] * SLC_BLOCK + jnp.arange(SLC_BLOCK)
        m_sel = (key_pos <= t[None, :, None, None, None]).reshape(B_, Q_BLOCK, N_KV, 1, n_top * SLC_BLOCK)
        s = jnp.einsum('bqghd,bqgnld->bqghnl', qb, k_sel).astype(jnp.float32)
        s = s.reshape(B_, Q_BLOCK, N_KV, HPG, n_top * SLC_BLOCK) * scale
        o_slc = jnp.einsum('bqghm,bqgmd->bqghd', masked_softmax(s, m_sel).astype(dt), v_sel)
        k_win = lax.dynamic_slice_in_dim(k_wp, qb_idx * Q_BLOCK, WINDOW + Q_BLOCK, axis=1)
        v_win = lax.dynamic_slice_in_dim(v_wp, qb_idx * Q_BLOCK, WINDOW + Q_BLOCK, axis=1)
        s_pos = (qb_idx * Q_BLOCK - WINDOW + jnp.arange(WINDOW + Q_BLOCK))[None, :]
        m_win = (s_pos <= t[:, None]) & (s_pos > t[:, None] - WINDOW) & (s_pos >= 0)
        s = jnp.einsum('bqghd,bkgd->bqghk', qb, k_win).astype(jnp.float32) * scale
        o_win = jnp.einsum('bqghk,bkgd->bqghd', masked_softmax(s, m_win[None, :, None, None, :]).astype(dt), v_win)
        return gb[..., 0:1] * o_cmp + gb[..., 1:2] * o_slc + gb[..., 2:3] * o_win

    q_blocks = q.reshape(B_, n_qb, Q_BLOCK, N_KV, HPG, HEAD_DIM).swapaxes(0, 1)
    g_blocks = gates.reshape(B_, n_qb, Q_BLOCK, N_KV, HPG, 3).swapaxes(0, 1)
    o = lax.map(block, (jnp.arange(n_qb), q_blocks, g_blocks))
    return o.swapaxes(0, 1).reshape(B_, S_, N_HEADS * HEAD_DIM) @ w_out


def _normal(key, shape, scale):
    return jax.random.normal(key, shape, jnp.float32) * scale


def _ln(key, n):
    kg, kb = jax.random.split(key)
    return 1.0 + _normal(kg, (n,), 0.01), _normal(kb, (n,), 0.01)


def _layer(key, pre, mixer):
    ks = jax.random.split(key, 20)
    p = {}
    p[pre + 'ffn1_w_in'] = _normal(ks[0], (D_MODEL, 2 * D_FF), D_MODEL ** -0.5)
    p[pre + 'ffn1_w_out'] = _normal(ks[1], (D_FF, D_MODEL), BETA * D_FF ** -0.5)
    p[pre + 'ln1_g'], p[pre + 'ln1_b'] = _ln(ks[2], D_MODEL)
    if mixer == 0:
        p[pre + 'gm_w_in'] = _normal(ks[3], (D_MODEL, 2 * GM_WIDTH), D_MODEL ** -0.5)
        p[pre + 'gm_ln_g'], p[pre + 'gm_ln_b'] = _ln(ks[4], GM_WIDTH)
        p[pre + 'gm_w_s'] = _normal(ks[5], (GM_GROUPS, GM_CHUNK, GM_CHUNK), GM_CHUNK ** -0.5)
        p[pre + 'gm_b_s'] = 1.0 + _normal(ks[6], (GM_GROUPS, GM_CHUNK), 0.1)
        p[pre + 'gm_w_out'] = _normal(ks[7], (GM_WIDTH, D_MODEL), BETA * GM_WIDTH ** -0.5)
    else:
        p[pre + 'nsa_w_in'] = _normal(ks[3], (D_MODEL, NSA_COLS), D_MODEL ** -0.5)
        p[pre + 'nsa_cmp_pe_k'] = _normal(ks[4], (CMP_BLOCK, HEAD_DIM), 0.1)
        p[pre + 'nsa_cmp_w1_k'] = _normal(ks[5], (CMP_BLOCK, HEAD_DIM, CMP_HIDDEN), (CMP_BLOCK * HEAD_DIM) ** -0.5)
        p[pre + 'nsa_cmp_w2_k'] = _normal(ks[6], (CMP_HIDDEN, HEAD_DIM), CMP_HIDDEN ** -0.5)
        p[pre + 'nsa_cmp_pe_v'] = _normal(ks[7], (CMP_BLOCK, HEAD_DIM), 0.1)
        p[pre + 'nsa_cmp_w1_v'] = _normal(ks[8], (CMP_BLOCK, HEAD_DIM, CMP_HIDDEN), (CMP_BLOCK * HEAD_DIM) ** -0.5)
        p[pre + 'nsa_cmp_w2_v'] = _normal(ks[9], (CMP_HIDDEN, HEAD_DIM), CMP_HIDDEN ** -0.5)
        p[pre + 'nsa_w_out'] = _normal(ks[10], (N_HEADS * HEAD_DIM, D_MODEL), BETA * (N_HEADS * HEAD_DIM) ** -0.5)
    p[pre + 'ln2_g'], p[pre + 'ln2_b'] = _ln(ks[11], D_MODEL)
    p[pre + 'ffn2_w_in'] = _normal(ks[12], (D_MODEL, 2 * D_FF), D_MODEL ** -0.5)
    p[pre + 'ffn2_w_out'] = _normal(ks[13], (D_FF, D_MODEL), BETA * D_FF ** -0.5)
    p[pre + 'ln3_g'], p[pre + 'ln3_b'] = _ln(ks[14], D_MODEL)
    return p


def setup_inputs(seed: int = 0) -> dict:
    key = jax.random.key(seed)
    kx, k0, k1 = jax.random.split(key, 3)
    out = {'x': jax.random.normal(kx, (BATCH, SEQ, D_MODEL), jnp.float32)}
    out.update(_layer(k0, 'l0_', 0))
    out.update(_layer(k1, 'l1_', 1))
    return out


def reference(x, l0_ffn1_w_in, l0_ffn1_w_out, l0_ln1_g, l0_ln1_b, l0_gm_w_in, l0_gm_ln_g, l0_gm_ln_b, l0_gm_w_s, l0_gm_b_s, l0_gm_w_out, l0_ln2_g, l0_ln2_b, l0_ffn2_w_in, l0_ffn2_w_out, l0_ln3_g, l0_ln3_b, l1_ffn1_w_in, l1_ffn1_w_out, l1_ln1_g, l1_ln1_b, l1_nsa_w_in, l1_nsa_cmp_pe_k, l1_nsa_cmp_w1_k, l1_nsa_cmp_w2_k, l1_nsa_cmp_pe_v, l1_nsa_cmp_w1_v, l1_nsa_cmp_w2_v, l1_nsa_w_out, l1_ln2_g, l1_ln2_b, l1_ffn2_w_in, l1_ffn2_w_out, l1_ln3_g, l1_ln3_b):
    layers = (
        dict(ffn1=(l0_ffn1_w_in, l0_ffn1_w_out), ln1=(l0_ln1_g, l0_ln1_b),
             mixer=(l0_gm_w_in, l0_gm_ln_g, l0_gm_ln_b, l0_gm_w_s, l0_gm_b_s, l0_gm_w_out),
             ln2=(l0_ln2_g, l0_ln2_b), ffn2=(l0_ffn2_w_in, l0_ffn2_w_out), ln3=(l0_ln3_g, l0_ln3_b)),
        dict(ffn1=(l1_ffn1_w_in, l1_ffn1_w_out), ln1=(l1_ln1_g, l1_ln1_b),
             mixer=(l1_nsa_w_in, l1_nsa_cmp_pe_k, l1_nsa_cmp_w1_k, l1_nsa_cmp_w2_k,
                    l1_nsa_cmp_pe_v, l1_nsa_cmp_w1_v, l1_nsa_cmp_w2_v, l1_nsa_w_out),
             ln2=(l1_ln2_g, l1_ln2_b), ffn2=(l1_ffn2_w_in, l1_ffn2_w_out), ln3=(l1_ln3_g, l1_ln3_b)),
    )
    for i in range(DEPTH):
        p = layers[i]
        x = layer_norm(ALPHA * x + 0.5 * swiglu(x, *p['ffn1']), *p['ln1'])
        if i % N_MIXERS == 0:
            mix = chunked_gmlp(x, *p['mixer'])
        else:
            mix = nsa(x, *p['mixer'])
        x = layer_norm(ALPHA * x + mix, *p['ln2'])
        x = layer_norm(ALPHA * x + 0.5 * swiglu(x, *p['ffn2']), *p['ln3'])
    return x
```

```python
import functools

import numpy as np
import jax
import jax.numpy as jnp
from jax import lax
from jax.experimental import pallas as pl
from jax.experimental.pallas import tpu as pltpu

F32 = jnp.float32
BF16 = jnp.bfloat16

D_MODEL = 1024
DEPTH = 2
ALPHA = (2 * DEPTH) ** 0.25
LN_EPS = 1e-5
D_FF = 2816
GM_WIDTH = 3 * D_MODEL
GM_CHUNK = 128
GM_GROUPS = 16
GM_GCH = GM_WIDTH // GM_GROUPS
HEAD_DIM = 64
N_HEADS = D_MODEL // HEAD_DIM
N_KV = 4
HPG = N_HEADS // N_KV
CMP_BLOCK = 32
CMP_STRIDE = 16
CMP_HIDDEN = 256
SLC_BLOCK = 64
SLC_TOP = 16
WINDOW = 512
ROPE_THETA = 10000.0
NEG = -1e30
KVW = N_KV * HEAD_DIM

LANES = 128
TQ = 128
SLC_CHUNK = 256
WIN_CHUNK = 128
SLC_HALF = 128
N_SLC_PAD = 2 * SLC_HALF
VMEM_LIMIT = 56 * 1024 * 1024


def _resident(shape, index_map):
    return pl.BlockSpec(shape, index_map, pipeline_mode=pl.Buffered(1))


def _layer_norm(y, g, b):
    mu = jnp.mean(y, axis=-1, keepdims=True)
    d = y - mu
    var = jnp.mean(d * d, axis=-1, keepdims=True)
    return d * lax.rsqrt(var + LN_EPS) * g + b


def _gelu(x):
    return 0.5 * x * (1.0 + jnp.tanh(0.7978845608028654 * (x + 0.044715 * (x * x * x))))


def _dot(a, b):
    return jnp.dot(a, b, preferred_element_type=F32)


FFN_COLS = 256


def _ffn_body(x_ref, win_ref, wout_ref, g_ref, b_ref, o_ref, a_ref):
    x = x_ref[...]
    xb = x.astype(BF16)
    for c in range(D_FF // FFN_COLS):
        lo = c * FFN_COLS
        gate = _dot(xb, win_ref[:, lo:lo + FFN_COLS])
        up = _dot(xb, win_ref[:, D_FF + lo:D_FF + lo + FFN_COLS])
        a_ref[:, lo:lo + FFN_COLS] = (gate * jax.nn.sigmoid(gate) * up).astype(BF16)
    y = _dot(a_ref[...], wout_ref[...])
    o_ref[...] = _layer_norm(ALPHA * x + 0.5 * y, g_ref[...], b_ref[...])


def _ffn_ln(x2, w_in, w_out, g, b, tm=512):
    m = x2.shape[0]
    return pl.pallas_call(
        _ffn_body,
        grid=(m // tm,),
        in_specs=[
            pl.BlockSpec((tm, D_MODEL), lambda i: (i, 0)),
            _resident((D_MODEL, 2 * D_FF), lambda i: (0, 0)),
            _resident((D_FF, D_MODEL), lambda i: (0, 0)),
            _resident((1, D_MODEL), lambda i: (0, 0)),
            _resident((1, D_MODEL), lambda i: (0, 0)),
        ],
        out_specs=pl.BlockSpec((tm, D_MODEL), lambda i: (i, 0)),
        out_shape=jax.ShapeDtypeStruct((m, D_MODEL), F32),
        scratch_shapes=[pltpu.VMEM((tm, D_FF), BF16)],
        compiler_params=pltpu.CompilerParams(
            dimension_semantics=("arbitrary",), vmem_limit_bytes=VMEM_LIMIT),
        name="ffn_ln",
    )(x2, w_in.astype(BF16), w_out.astype(BF16), g.reshape(1, -1), b.reshape(1, -1))


GM_PAIR = 2 * GM_GCH
GM_COLS = 256


def _gmlp_body(x_ref, win_ref, lng_ref, lnb_ref, ws_ref, bs_ref, wout_ref, g_ref, b_ref, o_ref,
               v_ref, a_ref, *, tm):
    x = x_ref[...]
    xb = x.astype(BF16)
    for c in range(GM_WIDTH // GM_COLS):
        lo = GM_WIDTH + c * GM_COLS
        v_ref[:, c * GM_COLS:(c + 1) * GM_COLS] = _gelu(_dot(xb, win_ref[:, lo:lo + GM_COLS]))
    v_ref[...] = _layer_norm(v_ref[...], lng_ref[...], lnb_ref[...])

    row = lax.broadcasted_iota(jnp.int32, (GM_CHUNK, GM_CHUNK), 0)
    col = lax.broadcasted_iota(jnp.int32, (GM_CHUNK, GM_CHUNK), 1)
    causal = row >= col
    first = lax.broadcasted_iota(jnp.int32, (1, GM_PAIR), 1) < GM_GCH
    for p in range(GM_GROUPS // 2):
        lo = p * GM_PAIR
        w0 = jnp.where(causal, ws_ref[2 * p], 0.0).astype(BF16)
        w1 = jnp.where(causal, ws_ref[2 * p + 1], 0.0).astype(BF16)
        wcat = jnp.concatenate([w0, w1], axis=1)
        u = _gelu(_dot(xb, win_ref[:, lo:lo + GM_PAIR]))
        bias = bs_ref[:, lo:lo + GM_PAIR]
        for n in range(tm // GM_CHUNK):
            r0, r1 = n * GM_CHUNK, (n + 1) * GM_CHUNK
            blk = v_ref[r0:r1, lo:lo + GM_PAIR]
            rhs = jnp.concatenate([jnp.where(first, blk, 0.0).astype(BF16),
                                   jnp.where(first, 0.0, blk).astype(BF16)], axis=0)
            sp = _dot(wcat, rhs) + bias
            a_ref[r0:r1, lo:lo + GM_PAIR] = (u[r0:r1] * sp).astype(BF16)
    y = _dot(a_ref[...], wout_ref[...])
    o_ref[...] = _layer_norm(ALPHA * x + y, g_ref[...], b_ref[...])


def _gmlp_ln(x2, w_in, ln_g, ln_b, w_s, b_s, w_out, g, b, tm=256):
    m = x2.shape[0]
    bias = jnp.repeat(b_s.T, GM_GCH, axis=1)
    const = lambda *shape: _resident(shape, lambda i: (0,) * len(shape))
    return pl.pallas_call(
        functools.partial(_gmlp_body, tm=tm),
        grid=(m // tm,),
        in_specs=[
            pl.BlockSpec((tm, D_MODEL), lambda i: (i, 0)),
            const(D_MODEL, 2 * GM_WIDTH),
            const(1, GM_WIDTH), const(1, GM_WIDTH),
            const(GM_GROUPS, GM_CHUNK, GM_CHUNK),
            const(GM_CHUNK, GM_WIDTH),
            const(GM_WIDTH, D_MODEL),
            const(1, D_MODEL), const(1, D_MODEL),
        ],
        out_specs=pl.BlockSpec((tm, D_MODEL), lambda i: (i, 0)),
        out_shape=jax.ShapeDtypeStruct((m, D_MODEL), F32),
        scratch_shapes=[pltpu.VMEM((tm, GM_WIDTH), F32), pltpu.VMEM((tm, GM_WIDTH), BF16)],
        compiler_params=pltpu.CompilerParams(
            dimension_semantics=("arbitrary",), vmem_limit_bytes=VMEM_LIMIT),
        name="gmlp_ln",
    )(x2, w_in.astype(BF16), ln_g.reshape(1, -1), ln_b.reshape(1, -1), w_s, bias,
      w_out.astype(BF16), g.reshape(1, -1), b.reshape(1, -1))


GATE_PAD = 16


def _rope_tables(pos):
    half = HEAD_DIM // 2
    freq = ROPE_THETA ** (-jnp.arange(half, dtype=F32) / half)
    ang = pos.astype(F32)[:, None] * freq
    cos, sin = jnp.cos(ang), jnp.sin(ang)
    return jnp.tile(cos, (1, 4)), jnp.tile(jnp.concatenate([-sin, sin], axis=1), (1, 2))


def _rope2(y, cos, sin):
    lane = lax.broadcasted_iota(jnp.int32, (1, LANES), 1)
    first = (lane % HEAD_DIM) < (HEAD_DIM // 2)
    partner = jnp.where(first, pltpu.roll(y, LANES - HEAD_DIM // 2, 1), pltpu.roll(y, HEAD_DIM // 2, 1))
    return y * cos + partner * sin


def _nsa_prep_body(x_ref, wq_ref, wkv_ref, wg_ref, cos_ref, sin_ref,
                   qt_ref, gt_ref, kc_ref, vc_ref, ks_ref, vst_ref, kw_ref, vwt_ref, *, tm):
    xb = x_ref[0].astype(BF16)
    cos, sin = cos_ref[...], sin_ref[...]
    low = lax.broadcasted_iota(jnp.int32, (1, LANES), 1) < HEAD_DIM
    scale = HEAD_DIM ** -0.5

    yq = _dot(xb, wq_ref[...])
    for i in range(D_MODEL // LANES):
        qtt = (_rope2(yq[:, i * LANES:(i + 1) * LANES], cos, sin) * scale).T
        qt_ref[0, 2 * i] = qtt[:HEAD_DIM].astype(BF16)
        qt_ref[0, 2 * i + 1] = qtt[HEAD_DIM:].astype(BF16)

    ykv = _dot(xb, wkv_ref[...])
    tile = lambda part, i: ykv[:, part * KVW + i * LANES:part * KVW + (i + 1) * LANES]
    for part, ref in ((0, kc_ref), (1, vc_ref)):
        for i in range(KVW // LANES):
            y = tile(part, i)
            ref[0, 2 * i] = y[:, :HEAD_DIM].astype(BF16)
            ref[0, 2 * i + 1] = pltpu.roll(y, HEAD_DIM, 1)[:, :HEAD_DIM].astype(BF16)
    for part, ref in ((2, ks_ref), (4, kw_ref)):
        for i in range(KVW // LANES):
            y = _rope2(tile(part, i), cos, sin)
            ref[0, 2 * i] = jnp.where(low, y, 0.0).astype(BF16)
            ref[0, 2 * i + 1] = jnp.where(low, pltpu.roll(y, HEAD_DIM, 1), 0.0).astype(BF16)
    ones_rows = jnp.where(lax.broadcasted_iota(jnp.int32, (HEAD_DIM, tm), 0) == 0, 1.0, 0.0)
    for part, ref, ck in ((3, vst_ref, SLC_CHUNK), (5, vwt_ref, WIN_CHUNK)):
        for i in range(KVW // LANES):
            vt = tile(part, i).T
            for j in range(2):
                blk = jnp.concatenate([vt[j * HEAD_DIM:(j + 1) * HEAD_DIM], ones_rows], axis=0).astype(BF16)
                for c in range(tm // ck):
                    ref[0, 2 * i + j, c] = blk[:, c * ck:(c + 1) * ck]

    sg = jax.nn.sigmoid(_dot(xb, wg_ref[...]))
    gt_ref[0] = sg.T[:N_KV * GATE_PAD]


def _nsa_prep(x, w_in, tm=512):
    b_, s_, _ = x.shape
    wq = w_in[:, :D_MODEL].astype(BF16)
    wkv = w_in[:, D_MODEL:D_MODEL + 6 * KVW].astype(BF16)
    wg = w_in[:, D_MODEL + 6 * KVW:].reshape(D_MODEL, N_KV, HPG * 3)
    wg = jnp.pad(wg, ((0, 0), (0, 0), (0, GATE_PAD - HPG * 3))).reshape(D_MODEL, N_KV * GATE_PAD)
    wg = jnp.pad(wg, ((0, 0), (0, LANES - N_KV * GATE_PAD))).astype(BF16)
    cos, sin = _rope_tables(jnp.arange(s_))
    const = lambda *shape: _resident(shape, lambda b, i: (0,) * len(shape))
    out_shapes = (
        jax.ShapeDtypeStruct((b_, N_HEADS, HEAD_DIM, s_), BF16),
        jax.ShapeDtypeStruct((b_, N_KV * GATE_PAD, s_), F32),
        jax.ShapeDtypeStruct((b_, N_KV, s_, HEAD_DIM), BF16),
        jax.ShapeDtypeStruct((b_, N_KV, s_, HEAD_DIM), BF16),
        jax.ShapeDtypeStruct((b_, N_KV, s_, LANES), BF16),
        jax.ShapeDtypeStruct((b_, N_KV, s_ // SLC_CHUNK, LANES, SLC_CHUNK), BF16),
        jax.ShapeDtypeStruct((b_, N_KV, s_, LANES), BF16),
        jax.ShapeDtypeStruct((b_, N_KV, s_ // WIN_CHUNK, LANES, WIN_CHUNK), BF16),
    )
    out_specs = (
        pl.BlockSpec((1, N_HEADS, HEAD_DIM, tm), lambda b, i: (b, 0, 0, i)),
        pl.BlockSpec((1, N_KV * GATE_PAD, tm), lambda b, i: (b, 0, i)),
        pl.BlockSpec((1, N_KV, tm, HEAD_DIM), lambda b, i: (b, 0, i, 0)),
        pl.BlockSpec((1, N_KV, tm, HEAD_DIM), lambda b, i: (b, 0, i, 0)),
        pl.BlockSpec((1, N_KV, tm, LANES), lambda b, i: (b, 0, i, 0)),
        pl.BlockSpec((1, N_KV, tm // SLC_CHUNK, LANES, SLC_CHUNK), lambda b, i: (b, 0, i, 0, 0)),
        pl.BlockSpec((1, N_KV, tm, LANES), lambda b, i: (b, 0, i, 0)),
        pl.BlockSpec((1, N_KV, tm // WIN_CHUNK, LANES, WIN_CHUNK), lambda b, i: (b, 0, i, 0, 0)),
    )
    return pl.pallas_call(
        functools.partial(_nsa_prep_body, tm=tm),
        grid=(b_, s_ // tm),
        in_specs=[
            pl.BlockSpec((1, tm, D_MODEL), lambda b, i: (b, i, 0)),
            const(D_MODEL, D_MODEL), const(D_MODEL, 6 * KVW), const(D_MODEL, LANES),
            pl.BlockSpec((tm, LANES), lambda b, i: (i, 0)),
            pl.BlockSpec((tm, LANES), lambda b, i: (i, 0)),
        ],
        out_specs=out_specs,
        out_shape=out_shapes,
        compiler_params=pltpu.CompilerParams(
            dimension_semantics=("arbitrary", "arbitrary"), vmem_limit_bytes=VMEM_LIMIT),
        name="nsa_prep",
    )(x, wq, wkv, wg, cos, sin)


def _compress_body(ck_ref, cv_ref, pek_ref, pev_ref, w1k_ref, w1v_ref, w2k_ref, w2v_ref, cos_ref, sin_ref,
                   kcmp_ref, vct_ref, *, n_rows):
    half = CMP_STRIDE * HEAD_DIM

    def mlp(c_ref, pe_ref, w1_ref, w2_ref):
        c = c_ref[0, 0]
        a = _dot(c, w1_ref[:half])
        bnext = pltpu.roll(_dot(c, w1_ref[half:]), n_rows - 1, 0)
        bias = _dot(pe_ref[...], w1_ref[...])[0:1]
        h = _gelu(a + bnext + bias)
        return _dot(h.astype(BF16), w2_ref[...])

    kcmp_ref[0, 0] = _rope2(mlp(ck_ref, pek_ref, w1k_ref, w2k_ref), cos_ref[...], sin_ref[...]).astype(BF16)
    vct_ref[0, 0] = mlp(cv_ref, pev_ref, w1v_ref, w2v_ref).T.astype(BF16)


def _compress(kc, vc, pe_k, w1_k, w2_k, pe_v, w1_v, w2_v):
    b_, g_, s_, _ = kc.shape
    n_rows = s_ // CMP_STRIDE
    feat = CMP_STRIDE * HEAD_DIM
    ck = kc.reshape(b_, g_, n_rows, feat)
    cv = vc.reshape(b_, g_, n_rows, feat)
    cos, sin = _rope_tables(CMP_STRIDE * jnp.arange(n_rows) + CMP_BLOCK - 1)
    pe = lambda p: jnp.pad(p.reshape(1, 2 * feat), ((0, 7), (0, 0))).astype(BF16)
    w1 = lambda w: w.reshape(2 * feat, CMP_HIDDEN).astype(BF16)
    w2 = lambda w: jnp.pad(w, ((0, 0), (0, LANES - HEAD_DIM))).astype(BF16)
    const = lambda *shape: _resident(shape, lambda b, g: (0,) * len(shape))
    blk = lambda *shape: pl.BlockSpec((1, 1) + shape, lambda b, g: (b, g, 0, 0))
    return pl.pallas_call(
        functools.partial(_compress_body, n_rows=n_rows),
        grid=(b_, g_),
        in_specs=[
            blk(n_rows, feat), blk(n_rows, feat),
            const(8, 2 * feat), const(8, 2 * feat),
            const(2 * feat, CMP_HIDDEN), const(2 * feat, CMP_HIDDEN),
            const(CMP_HIDDEN, LANES), const(CMP_HIDDEN, LANES),
            const(n_rows, LANES), const(n_rows, LANES),
        ],
        out_specs=(blk(n_rows, LANES), blk(LANES, n_rows)),
        out_shape=(jax.ShapeDtypeStruct((b_, g_, n_rows, LANES), BF16),
                   jax.ShapeDtypeStruct((b_, g_, LANES, n_rows), BF16)),
        compiler_params=pltpu.CompilerParams(
            dimension_semantics=("arbitrary", "arbitrary"), vmem_limit_bytes=VMEM_LIMIT),
        name="nsa_compress",
    )(ck, cv, pe(pe_k), pe(pe_v), w1(w1_k), w1(w1_v), w2(w2_k), w2(w2_v), cos, sin)


def _block_sum_matrix(n_cmp_rows):
    r = SLC_BLOCK // CMP_STRIDE
    j = np.arange(N_SLC_PAD)[:, None]
    n = np.arange(n_cmp_rows)[None, :]
    return ((n >= r * j - 1) & (n <= r * j + r - 1) & (n < n_cmp_rows - 1)).astype(np.float32)


def _nsa_attn_body(qt_ref, gt_ref, kcmp_ref, vct_ref, ks_ref, vst_ref, kw_ref, vwt_ref, mt_ref,
                   o_ref, qa_ref, *, n_cmp_rows):
    t_idx = pl.program_id(2)
    t0 = t_idx * TQ
    width = HPG * TQ
    q4 = jnp.concatenate([qt_ref[0, h] for h in range(HPG)], axis=1)
    zeros = jnp.zeros((HEAD_DIM, width), BF16)
    q4p = jnp.concatenate([q4, zeros], axis=0)
    tq = t0 + lax.broadcasted_iota(jnp.int32, (1, width), 1) % TQ
    tq1 = t0 + lax.broadcasted_iota(jnp.int32, (1, TQ), 1)

    s = _dot(kcmp_ref[0, 0], q4p)
    n_i = lax.broadcasted_iota(jnp.int32, (n_cmp_rows, 1), 0)
    valid = (n_i * CMP_STRIDE + (CMP_BLOCK - 1)) <= tq
    s = jnp.where(valid, s, NEG)
    m = jnp.max(s, axis=0, keepdims=True)
    p = jnp.where(valid, jnp.exp(s - m), 0.0)
    l = jnp.sum(p, axis=0, keepdims=True)
    pn = p * jnp.where(l > 0.0, 1.0 / jnp.where(l > 0.0, l, 1.0), 0.0)
    o_cmp = _dot(vct_ref[0, 0], pn.astype(BF16))

    ps = pn[:, 0:TQ]
    for h in range(1, HPG):
        ps = ps + pn[:, h * TQ:(h + 1) * TQ]
    hi = ps.astype(BF16)
    r1 = ps - hi.astype(F32)
    mid = r1.astype(BF16)
    lo = (r1 - mid.astype(F32)).astype(BF16)
    mt = mt_ref[...]
    imp = _dot(mt, hi) + _dot(mt, mid) + _dot(mt, lo)
    j_i = lax.broadcasted_iota(jnp.int32, (N_SLC_PAD, 1), 0)
    cur = tq1 // SLC_BLOCK
    forced = (j_i == 0) | (j_i == cur) | (j_i == cur - 1)
    val = jnp.where(forced, 1e9, jnp.where(j_i <= cur, imp, -1e9))
    sel = jnp.zeros((N_SLC_PAD, TQ), jnp.int32)
    for _ in range(SLC_TOP):
        vmax = jnp.max(val, axis=0, keepdims=True)
        jmin = jnp.min(jnp.where(val == vmax, j_i, N_SLC_PAD), axis=0, keepdims=True)
        pick = j_i == jmin
        sel = jnp.where(pick, 1, sel)
        val = jnp.where(pick, -3e38, val)
    bias = jnp.where((sel == 1) & (j_i <= cur), 0.0, NEG).astype(BF16)
    for half in range(2):
        bh = bias[half * SLC_HALF:(half + 1) * SLC_HALF]
        qa_ref[half] = jnp.concatenate([jnp.concatenate([bh] * HPG, axis=1), q4, zeros], axis=0)

    row_k = lax.broadcasted_iota(jnp.int32, (SLC_CHUNK, 1), 0)
    col_b = lax.broadcasted_iota(jnp.int32, (1, SLC_HALF), 1)
    chunks_per_half = SLC_HALF * SLC_BLOCK // SLC_CHUNK

    def slc_step(c, carry):
        m_i, acc = carry
        base = pl.multiple_of(c * SLC_CHUNK, SLC_CHUNK)
        half = c // chunks_per_half
        key = base + row_k
        onehot = jnp.where((key // SLC_BLOCK - half * SLC_HALF) == col_b, 1.0, 0.0).astype(BF16)
        rhs = jnp.concatenate([onehot, ks_ref[0, 0, pl.ds(base, SLC_CHUNK), :]], axis=1)
        sc = _dot(rhs, qa_ref[half])
        sc = jnp.where(key <= tq, sc, NEG)
        m_new = jnp.maximum(m_i, jnp.max(sc, axis=0, keepdims=True))
        pe = jnp.exp(sc - m_new)
        acc = jnp.exp(m_i - m_new) * acc + _dot(vst_ref[0, 0, c], pe.astype(BF16))
        return m_new, acc

    _, acc = lax.fori_loop(0, t0 // SLC_CHUNK + 1, slc_step,
                           (jnp.full((1, width), NEG, F32), jnp.zeros((LANES, width), F32)))
    o_slc = acc[:HEAD_DIM] * (1.0 / acc[HEAD_DIM:HEAD_DIM + 1])

    row_w = lax.broadcasted_iota(jnp.int32, (WIN_CHUNK, 1), 0)
    n_win = WINDOW // WIN_CHUNK + 1
    sw, cw = [], []
    for w in range(n_win):
        c_true = t_idx - (n_win - 1) + w
        c_load = jnp.maximum(c_true, 0)
        cw.append(c_load)
        kw = kw_ref[0, 0, pl.ds(pl.multiple_of(c_load * WIN_CHUNK, WIN_CHUNK), WIN_CHUNK), :]
        key = c_true * WIN_CHUNK + row_w
        ok = (key <= tq) & (key > tq - WINDOW) & (key >= 0)
        sw.append(jnp.where(ok, _dot(kw, q4p), NEG))
    m = sw[0].max(axis=0, keepdims=True)
    for w in range(1, n_win):
        m = jnp.maximum(m, sw[w].max(axis=0, keepdims=True))
    acc = jnp.zeros((LANES, width), F32)
    for w in range(n_win):
        acc = acc + _dot(vwt_ref[0, 0, cw[w]], jnp.exp(sw[w] - m).astype(BF16))
    o_win = acc[:HEAD_DIM] * (1.0 / acc[HEAD_DIM:HEAD_DIM + 1])

    gates = gt_ref[0]
    outs = []
    for h in range(HPG):
        cols = slice(h * TQ, (h + 1) * TQ)
        outs.append(gates[3 * h:3 * h + 1] * o_cmp[:HEAD_DIM, cols]
                    + gates[3 * h + 1:3 * h + 2] * o_slc[:, cols]
                    + gates[3 * h + 2:3 * h + 3] * o_win[:, cols])
    o_ref[0] = jnp.concatenate(outs, axis=0).T


def _nsa_attn(qt, gt, kcmp, vct, ks, vst, kw, vwt):
    b_, _, _, s_ = qt.shape
    n_cmp_rows = kcmp.shape[2]
    assert s_ % SLC_CHUNK == 0 and s_ // SLC_BLOCK <= N_SLC_PAD
    mt = jnp.asarray(_block_sum_matrix(n_cmp_rows), BF16)
    per_group = lambda *shape: _resident(
        (1, 1) + shape, lambda b, g, t: (b, g) + (0,) * len(shape))
    return pl.pallas_call(
        functools.partial(_nsa_attn_body, n_cmp_rows=n_cmp_rows),
        grid=(b_, N_KV, s_ // TQ),
        in_specs=[
            pl.BlockSpec((1, HPG, HEAD_DIM, TQ), lambda b, g, t: (b, g, 0, t)),
            pl.BlockSpec((1, GATE_PAD, TQ), lambda b, g, t: (b, g, t)),
            per_group(n_cmp_rows, LANES),
            per_group(LANES, n_cmp_rows),
            per_group(s_, LANES),
            per_group(s_ // SLC_CHUNK, LANES, SLC_CHUNK),
            per_group(s_, LANES),
            per_group(s_ // WIN_CHUNK, LANES, WIN_CHUNK),
            _resident((N_SLC_PAD, n_cmp_rows), lambda b, g, t: (0, 0)),
        ],
        out_specs=pl.BlockSpec((1, TQ, HPG * HEAD_DIM), lambda b, g, t: (b, t, g)),
        out_shape=jax.ShapeDtypeStruct((b_, s_, N_HEADS * HEAD_DIM), F32),
        scratch_shapes=[pltpu.VMEM((2, 2 * LANES, HPG * TQ), BF16)],
        compiler_params=pltpu.CompilerParams(
            dimension_semantics=("arbitrary", "arbitrary", "arbitrary"), vmem_limit_bytes=VMEM_LIMIT),
        name="nsa_attn",
    )(qt, gt, kcmp, vct, ks, vst, kw, vwt, mt)


def _proj_body(o_ref, x_ref, w_ref, g_ref, b_ref, y_ref):
    y = _dot(o_ref[...].astype(BF16), w_ref[...])
    y_ref[...] = _layer_norm(ALPHA * x_ref[...] + y, g_ref[...], b_ref[...])


def _proj_ln(o2, x2, w, g, b, tm=512):
    m = x2.shape[0]
    return pl.pallas_call(
        _proj_body,
        grid=(m // tm,),
        in_specs=[
            pl.BlockSpec((tm, D_MODEL), lambda i: (i, 0)),
            pl.BlockSpec((tm, D_MODEL), lambda i: (i, 0)),
            _resident((D_MODEL, D_MODEL), lambda i: (0, 0)),
            _resident((1, D_MODEL), lambda i: (0, 0)),
            _resident((1, D_MODEL), lambda i: (0, 0)),
        ],
        out_specs=pl.BlockSpec((tm, D_MODEL), lambda i: (i, 0)),
        out_shape=jax.ShapeDtypeStruct((m, D_MODEL), F32),
        compiler_params=pltpu.CompilerParams(
            dimension_semantics=("arbitrary",), vmem_limit_bytes=VMEM_LIMIT),
        name="proj_ln",
    )(o2, x2, w.astype(BF16), g.reshape(1, -1), b.reshape(1, -1))


def _nsa_mixer_ln(x, w_in, pe_k, w1_k, w2_k, pe_v, w1_v, w2_v, w_out, g, b):
    b_, s_, _ = x.shape
    qt, gt, kc, vc, ks, vst, kw, vwt = _nsa_prep(x, w_in)
    kcmp, vct = _compress(kc, vc, pe_k, w1_k, w2_k, pe_v, w1_v, w2_v)
    o = _nsa_attn(qt, gt, kcmp, vct, ks, vst, kw, vwt)
    return _proj_ln(o.reshape(b_ * s_, -1), x.reshape(b_ * s_, -1), w_out, g, b).reshape(x.shape)


def kernel(x, l0_ffn1_w_in, l0_ffn1_w_out, l0_ln1_g, l0_ln1_b, l0_gm_w_in, l0_gm_ln_g, l0_gm_ln_b, l0_gm_w_s, l0_gm_b_s, l0_gm_w_out, l0_ln2_g, l0_ln2_b, l0_ffn2_w_in, l0_ffn2_w_out, l0_ln3_g, l0_ln3_b, l1_ffn1_w_in, l1_ffn1_w_out, l1_ln1_g, l1_ln1_b, l1_nsa_w_in, l1_nsa_cmp_pe_k, l1_nsa_cmp_w1_k, l1_nsa_cmp_w2_k, l1_nsa_cmp_pe_v, l1_nsa_cmp_w1_v, l1_nsa_cmp_w2_v, l1_nsa_w_out, l1_ln2_g, l1_ln2_b, l1_ffn2_w_in, l1_ffn2_w_out, l1_ln3_g, l1_ln3_b):
    b_, s_, d_ = x.shape
    h = x.reshape(b_ * s_, d_)
    h = _ffn_ln(h, l0_ffn1_w_in, l0_ffn1_w_out, l0_ln1_g, l0_ln1_b)
    h = _gmlp_ln(h, l0_gm_w_in, l0_gm_ln_g, l0_gm_ln_b, l0_gm_w_s, l0_gm_b_s, l0_gm_w_out, l0_ln2_g, l0_ln2_b)
    h = _ffn_ln(h, l0_ffn2_w_in, l0_ffn2_w_out, l0_ln3_g, l0_ln3_b)
    h = _ffn_ln(h, l1_ffn1_w_in, l1_ffn1_w_out, l1_ln1_g, l1_ln1_b)
    h = _nsa_mixer_ln(h.reshape(b_, s_, d_), l1_nsa_w_in, l1_nsa_cmp_pe_k, l1_nsa_cmp_w1_k, l1_nsa_cmp_w2_k,
                      l1_nsa_cmp_pe_v, l1_nsa_cmp_w1_v, l1_nsa_cmp_w2_v, l1_nsa_w_out, l1_ln2_g, l1_ln2_b)
    h = _ffn_ln(h.reshape(b_ * s_, d_), l1_ffn2_w_in, l1_ffn2_w_out, l1_ln3_g, l1_ln3_b)
    return h.reshape(b_, s_, d_)
```

```python
import functools

import jax
import jax.numpy as jnp
from jax import lax
from jax.experimental import pallas as pl
from jax.experimental.pallas import tpu as pltpu

F32 = jnp.float32
BF16 = jnp.bfloat16

D_MODEL = 1024
DEPTH = 2
ALPHA = (2 * DEPTH) ** 0.25
LN_EPS = 1e-5
D_FF = 2816
GM_WIDTH = 3 * D_MODEL
GM_CHUNK = 128
GM_GROUPS = 16
GM_GCH = GM_WIDTH // GM_GROUPS
HEAD_DIM = 64
N_HEADS = D_MODEL // HEAD_DIM
N_KV = 4
HPG = N_HEADS // N_KV
CMP_BLOCK = 32
CMP_STRIDE = 16
CMP_HIDDEN = 256
SLC_BLOCK = 64
SLC_TOP = 16
WINDOW = 512
ROPE_THETA = 10000.0
NEG = -1e30
KVW = N_KV * HEAD_DIM

LANES = 128
TQ = 128
SLC_CHUNK = 512
WIN_CHUNK = 128
CMP_ROWGRP = TQ // CMP_STRIDE
CMP_VARIANTS = 4
SLC_HALF = 128
N_SLC_PAD = 2 * SLC_HALF
V_ROWS = 80
LOG2E = 1.4426950408889634
VMEM_LIMIT = 56 * 1024 * 1024


def _resident(shape, index_map):
    return pl.BlockSpec(shape, index_map, pipeline_mode=pl.Buffered(1))


def _layer_norm(y, g, b):
    mu = jnp.mean(y, axis=-1, keepdims=True)
    d = y - mu
    var = jnp.mean(d * d, axis=-1, keepdims=True)
    return d * lax.rsqrt(var + LN_EPS) * g + b


def _gelu(x):
    return 0.5 * x * (1.0 + jnp.tanh(0.7978845608028654 * (x + 0.044715 * (x * x * x))))


def _dot(a, b):
    return jnp.dot(a, b, preferred_element_type=F32)


FFN_COLS = 256


def _ffn_body(x_ref, win_ref, wout_ref, g_ref, b_ref, o_ref, a_ref):
    x = x_ref[...]
    xb = x.astype(BF16)
    for c in range(D_FF // FFN_COLS):
        lo = c * FFN_COLS
        gate = _dot(xb, win_ref[:, lo:lo + FFN_COLS])
        up = _dot(xb, win_ref[:, D_FF + lo:D_FF + lo + FFN_COLS])
        a_ref[:, lo:lo + FFN_COLS] = (gate * jax.nn.sigmoid(gate) * up).astype(BF16)
    y = _dot(a_ref[...], wout_ref[...])
    o_ref[...] = _layer_norm(ALPHA * x + 0.5 * y, g_ref[...], b_ref[...])


def _ffn_ln(x2, w_in, w_out, g, b, tm=512):
    m = x2.shape[0]
    return pl.pallas_call(
        _ffn_body,
        grid=(m // tm,),
        in_specs=[
            pl.BlockSpec((tm, D_MODEL), lambda i: (i, 0)),
            _resident((D_MODEL, 2 * D_FF), lambda i: (0, 0)),
            _resident((D_FF, D_MODEL), lambda i: (0, 0)),
            _resident((1, D_MODEL), lambda i: (0, 0)),
            _resident((1, D_MODEL), lambda i: (0, 0)),
        ],
        out_specs=pl.BlockSpec((tm, D_MODEL), lambda i: (i, 0)),
        out_shape=jax.ShapeDtypeStruct((m, D_MODEL), F32),
        scratch_shapes=[pltpu.VMEM((tm, D_FF), BF16)],
        compiler_params=pltpu.CompilerParams(
            dimension_semantics=("arbitrary",), vmem_limit_bytes=VMEM_LIMIT),
        name="ffn_ln",
    )(x2, w_in.astype(BF16), w_out.astype(BF16), g.reshape(1, -1), b.reshape(1, -1))


GM_PAIR = 2 * GM_GCH
GM_COLS = 256


def _gmlp_body(x_ref, win_ref, lng_ref, lnb_ref, ws_ref, bs_ref, wout_ref, g_ref, b_ref, o_ref,
               v_ref, a_ref, *, tm):
    x = x_ref[...]
    xb = x.astype(BF16)
    for c in range(GM_WIDTH // GM_COLS):
        lo = GM_WIDTH + c * GM_COLS
        v_ref[:, c * GM_COLS:(c + 1) * GM_COLS] = _gelu(_dot(xb, win_ref[:, lo:lo + GM_COLS]))
    v_ref[...] = _layer_norm(v_ref[...], lng_ref[...], lnb_ref[...])

    row = lax.broadcasted_iota(jnp.int32, (GM_CHUNK, GM_CHUNK), 0)
    col = lax.broadcasted_iota(jnp.int32, (GM_CHUNK, GM_CHUNK), 1)
    causal = row >= col
    first = lax.broadcasted_iota(jnp.int32, (1, GM_PAIR), 1) < GM_GCH
    for p in range(GM_GROUPS // 2):
        lo = p * GM_PAIR
        w0 = jnp.where(causal, ws_ref[2 * p], 0.0).astype(BF16)
        w1 = jnp.where(causal, ws_ref[2 * p + 1], 0.0).astype(BF16)
        wcat = jnp.concatenate([w0, w1], axis=1)
        u = _gelu(_dot(xb, win_ref[:, lo:lo + GM_PAIR]))
        bias = bs_ref[:, lo:lo + GM_PAIR]
        for n in range(tm // GM_CHUNK):
            r0, r1 = n * GM_CHUNK, (n + 1) * GM_CHUNK
            blk = v_ref[r0:r1, lo:lo + GM_PAIR]
            rhs = jnp.concatenate([jnp.where(first, blk, 0.0).astype(BF16),
                                   jnp.where(first, 0.0, blk).astype(BF16)], axis=0)
            sp = _dot(wcat, rhs) + bias
            a_ref[r0:r1, lo:lo + GM_PAIR] = (u[r0:r1] * sp).astype(BF16)
    y = _dot(a_ref[...], wout_ref[...])
    o_ref[...] = _layer_norm(ALPHA * x + y, g_ref[...], b_ref[...])


def _gmlp_ln(x2, w_in, ln_g, ln_b, w_s, b_s, w_out, g, b, tm=256):
    m = x2.shape[0]
    bias = jnp.repeat(b_s.T, GM_GCH, axis=1)
    const = lambda *shape: _resident(shape, lambda i: (0,) * len(shape))
    return pl.pallas_call(
        functools.partial(_gmlp_body, tm=tm),
        grid=(m // tm,),
        in_specs=[
            pl.BlockSpec((tm, D_MODEL), lambda i: (i, 0)),
            const(D_MODEL, 2 * GM_WIDTH),
            const(1, GM_WIDTH), const(1, GM_WIDTH),
            const(GM_GROUPS, GM_CHUNK, GM_CHUNK),
            const(GM_CHUNK, GM_WIDTH),
            const(GM_WIDTH, D_MODEL),
            const(1, D_MODEL), const(1, D_MODEL),
        ],
        out_specs=pl.BlockSpec((tm, D_MODEL), lambda i: (i, 0)),
        out_shape=jax.ShapeDtypeStruct((m, D_MODEL), F32),
        scratch_shapes=[pltpu.VMEM((tm, GM_WIDTH), F32), pltpu.VMEM((tm, GM_WIDTH), BF16)],
        compiler_params=pltpu.CompilerParams(
            dimension_semantics=("arbitrary",), vmem_limit_bytes=VMEM_LIMIT),
        name="gmlp_ln",
    )(x2, w_in.astype(BF16), ln_g.reshape(1, -1), ln_b.reshape(1, -1), w_s, bias,
      w_out.astype(BF16), g.reshape(1, -1), b.reshape(1, -1))


GATE_PAD = 16


def _rope_tables(pos):
    half = HEAD_DIM // 2
    freq = ROPE_THETA ** (-jnp.arange(half, dtype=F32) / half)
    ang = pos.astype(F32)[:, None] * freq
    cos, sin = jnp.cos(ang), jnp.sin(ang)
    return jnp.tile(cos, (1, 4)), jnp.tile(jnp.concatenate([-sin, sin], axis=1), (1, 2))


def _rope2(y, cos, sin):
    lane = lax.broadcasted_iota(jnp.int32, (1, LANES), 1)
    first = (lane % HEAD_DIM) < (HEAD_DIM // 2)
    partner = jnp.where(first, pltpu.roll(y, LANES - HEAD_DIM // 2, 1), pltpu.roll(y, HEAD_DIM // 2, 1))
    return y * cos + partner * sin


def _nsa_prep_body(x_ref, wq_ref, wkv_ref, wg_ref, cos_ref, sin_ref,
                   qt_ref, gt_ref, kc_ref, vc_ref, ks_ref, vst_ref, kw_ref, vwt_ref, *, tm):
    xb = x_ref[0].astype(BF16)
    cos, sin = cos_ref[...], sin_ref[...]
    lane = lax.broadcasted_iota(jnp.int32, (1, LANES), 1)
    low = lane < HEAD_DIM
    scale = HEAD_DIM ** -0.5 * LOG2E

    yq = _dot(xb, wq_ref[...])
    for i in range(D_MODEL // LANES):
        qtt = (_rope2(yq[:, i * LANES:(i + 1) * LANES], cos, sin) * scale).T
        qt_ref[0, 2 * i] = qtt[:HEAD_DIM].astype(BF16)
        qt_ref[0, 2 * i + 1] = qtt[HEAD_DIM:].astype(BF16)

    ykv = _dot(xb, wkv_ref[...])
    tile = lambda part, i: ykv[:, part * KVW + i * LANES:part * KVW + (i + 1) * LANES]
    for part, ref in ((0, kc_ref), (1, vc_ref)):
        for i in range(KVW // LANES):
            y = tile(part, i)
            ref[0, 2 * i] = y[:, :HEAD_DIM].astype(BF16)
            ref[0, 2 * i + 1] = pltpu.roll(y, HEAD_DIM, 1)[:, :HEAD_DIM].astype(BF16)
    pos = pl.program_id(1) * tm + lax.broadcasted_iota(jnp.int32, (tm, 1), 0)
    onehot = jnp.where((pos // SLC_BLOCK) % SLC_HALF == lane, 1.0, 0.0).astype(BF16)
    for part, ref in ((2, ks_ref), (4, kw_ref)):
        for i in range(KVW // LANES):
            y = _rope2(tile(part, i), cos, sin)
            for j, yj in enumerate((y, pltpu.roll(y, HEAD_DIM, 1))):
                kj = jnp.where(low, yj, 0.0).astype(BF16)
                ref[0, 2 * i + j] = jnp.concatenate([onehot, kj], axis=1) if ref is ks_ref else kj
    ones_rows = jnp.where(lax.broadcasted_iota(jnp.int32, (V_ROWS - HEAD_DIM, tm), 0) == 0, 1.0, 0.0)
    for part, ref, ck in ((3, vst_ref, SLC_CHUNK), (5, vwt_ref, WIN_CHUNK)):
        for i in range(KVW // LANES):
            vt = tile(part, i).T
            for j in range(2):
                blk = jnp.concatenate([vt[j * HEAD_DIM:(j + 1) * HEAD_DIM], ones_rows], axis=0).astype(BF16)
                for c in range(tm // ck):
                    ref[0, 2 * i + j, c] = blk[:, c * ck:(c + 1) * ck]

    sg = jax.nn.sigmoid(_dot(xb, wg_ref[...]))
    gt_ref[0] = sg.T[:N_KV * GATE_PAD]


def _nsa_prep(x, w_in, tm=512):
    b_, s_, _ = x.shape
    wq = w_in[:, :D_MODEL].astype(BF16)
    wkv = w_in[:, D_MODEL:D_MODEL + 6 * KVW].astype(BF16)
    wg = w_in[:, D_MODEL + 6 * KVW:].reshape(D_MODEL, N_KV, HPG * 3)
    wg = jnp.pad(wg, ((0, 0), (0, 0), (0, GATE_PAD - HPG * 3))).reshape(D_MODEL, N_KV * GATE_PAD)
    wg = jnp.pad(wg, ((0, 0), (0, LANES - N_KV * GATE_PAD))).astype(BF16)
    cos, sin = _rope_tables(jnp.arange(s_))
    const = lambda *shape: _resident(shape, lambda b, i: (0,) * len(shape))
    out_shapes = (
        jax.ShapeDtypeStruct((b_, N_HEADS, HEAD_DIM, s_), BF16),
        jax.ShapeDtypeStruct((b_, N_KV * GATE_PAD, s_), F32),
        jax.ShapeDtypeStruct((b_, N_KV, s_, HEAD_DIM), BF16),
        jax.ShapeDtypeStruct((b_, N_KV, s_, HEAD_DIM), BF16),
        jax.ShapeDtypeStruct((b_, N_KV, s_, 2 * LANES), BF16),
        jax.ShapeDtypeStruct((b_, N_KV, s_ // SLC_CHUNK, V_ROWS, SLC_CHUNK), BF16),
        jax.ShapeDtypeStruct((b_, N_KV, s_, LANES), BF16),
        jax.ShapeDtypeStruct((b_, N_KV, s_ // WIN_CHUNK, V_ROWS, WIN_CHUNK), BF16),
    )
    out_specs = (
        pl.BlockSpec((1, N_HEADS, HEAD_DIM, tm), lambda b, i: (b, 0, 0, i)),
        pl.BlockSpec((1, N_KV * GATE_PAD, tm), lambda b, i: (b, 0, i)),
        pl.BlockSpec((1, N_KV, tm, HEAD_DIM), lambda b, i: (b, 0, i, 0)),
        pl.BlockSpec((1, N_KV, tm, HEAD_DIM), lambda b, i: (b, 0, i, 0)),
        pl.BlockSpec((1, N_KV, tm, 2 * LANES), lambda b, i: (b, 0, i, 0)),
        pl.BlockSpec((1, N_KV, tm // SLC_CHUNK, V_ROWS, SLC_CHUNK), lambda b, i: (b, 0, i, 0, 0)),
        pl.BlockSpec((1, N_KV, tm, LANES), lambda b, i: (b, 0, i, 0)),
        pl.BlockSpec((1, N_KV, tm // WIN_CHUNK, V_ROWS, WIN_CHUNK), lambda b, i: (b, 0, i, 0, 0)),
    )
    return pl.pallas_call(
        functools.partial(_nsa_prep_body, tm=tm),
        grid=(b_, s_ // tm),
        in_specs=[
            pl.BlockSpec((1, tm, D_MODEL), lambda b, i: (b, i, 0)),
            const(D_MODEL, D_MODEL), const(D_MODEL, 6 * KVW), const(D_MODEL, LANES),
            pl.BlockSpec((tm, LANES), lambda b, i: (i, 0)),
            pl.BlockSpec((tm, LANES), lambda b, i: (i, 0)),
        ],
        out_specs=out_specs,
        out_shape=out_shapes,
        compiler_params=pltpu.CompilerParams(
            dimension_semantics=("arbitrary", "arbitrary"), vmem_limit_bytes=VMEM_LIMIT),
        name="nsa_prep",
    )(x, wq, wkv, wg, cos, sin)


def _compress_body(ck_ref, cv_ref, pek_ref, pev_ref, w1k_ref, w1v_ref, w2k_ref, w2v_ref, cos_ref, sin_ref,
                   kcmp_ref, vct_ref, *, n_rows):
    half = CMP_STRIDE * HEAD_DIM

    def mlp(c_ref, pe_ref, w1_ref, w2_ref):
        c = c_ref[0, 0]
        a = _dot(c, w1_ref[:half])
        bnext = pltpu.roll(_dot(c, w1_ref[half:]), n_rows - 1, 0)
        bias = _dot(pe_ref[...], w1_ref[...])[0:1]
        h = _gelu(a + bnext + bias)
        return _dot(h.astype(BF16), w2_ref[...])

    kc = _rope2(mlp(ck_ref, pek_ref, w1k_ref, w2k_ref), cos_ref[...], sin_ref[...]).astype(BF16)
    grp = lax.broadcasted_iota(jnp.int32, (n_rows, 1), 0) // CMP_ROWGRP
    onehot = jnp.where(grp == lax.broadcasted_iota(jnp.int32, (1, LANES), 1), 1.0, 0.0).astype(BF16)
    kcmp_ref[0, 0] = jnp.concatenate([onehot, kc], axis=1)
    vt = mlp(cv_ref, pev_ref, w1v_ref, w2v_ref).T
    ones_rows = jnp.where(lax.broadcasted_iota(jnp.int32, (V_ROWS - HEAD_DIM, n_rows), 0) == 0, 1.0, 0.0)
    vct_ref[0, 0] = jnp.concatenate([vt[:HEAD_DIM], ones_rows], axis=0).astype(BF16)


def _compress(kc, vc, pe_k, w1_k, w2_k, pe_v, w1_v, w2_v):
    b_, g_, s_, _ = kc.shape
    n_rows = s_ // CMP_STRIDE
    feat = CMP_STRIDE * HEAD_DIM
    ck = kc.reshape(b_, g_, n_rows, feat)
    cv = vc.reshape(b_, g_, n_rows, feat)
    cos, sin = _rope_tables(CMP_STRIDE * jnp.arange(n_rows) + CMP_BLOCK - 1)
    pe = lambda p: jnp.pad(p.reshape(1, 2 * feat), ((0, 7), (0, 0))).astype(BF16)
    w1 = lambda w: w.reshape(2 * feat, CMP_HIDDEN).astype(BF16)
    w2 = lambda w: jnp.pad(w, ((0, 0), (0, LANES - HEAD_DIM))).astype(BF16)
    const = lambda *shape: _resident(shape, lambda b, g: (0,) * len(shape))
    blk = lambda *shape: pl.BlockSpec((1, 1) + shape, lambda b, g: (b, g, 0, 0))
    return pl.pallas_call(
        functools.partial(_compress_body, n_rows=n_rows),
        grid=(b_, g_),
        in_specs=[
            blk(n_rows, feat), blk(n_rows, feat),
            const(8, 2 * feat), const(8, 2 * feat),
            const(2 * feat, CMP_HIDDEN), const(2 * feat, CMP_HIDDEN),
            const(CMP_HIDDEN, LANES), const(CMP_HIDDEN, LANES),
            const(n_rows, LANES), const(n_rows, LANES),
        ],
        out_specs=(blk(n_rows, 2 * LANES), blk(V_ROWS, n_rows)),
        out_shape=(jax.ShapeDtypeStruct((b_, g_, n_rows, 2 * LANES), BF16),
                   jax.ShapeDtypeStruct((b_, g_, V_ROWS, n_rows), BF16)),
        compiler_params=pltpu.CompilerParams(
            dimension_semantics=("arbitrary", "arbitrary"), vmem_limit_bytes=VMEM_LIMIT),
        name="nsa_compress",
    )(ck, cv, pe(pe_k), pe(pe_v), w1(w1_k), w1(w1_v), w2(w2_k), w2(w2_v), cos, sin)


def _top_blocks(imp, cur, rows):
    j_i = lax.broadcasted_iota(jnp.int32, (rows, 1), 0)
    forced = (j_i == 0) | (j_i == cur) | (j_i == cur - 1)
    excluded, taken = -3e38, -2e38
    val = jnp.where(forced | (j_i > cur), excluded, imp)
    for _ in range(SLC_TOP - 3):
        vmax = jnp.max(val, axis=0, keepdims=True)
        jmin = jnp.min(jnp.where(val == vmax, j_i, rows), axis=0, keepdims=True)
        val = jnp.where(j_i == jmin, taken, val)
    return jnp.where((forced | (val == taken)) & (j_i <= cur), 0.0, NEG)


def _nsa_attn_body(qt_ref, gt_ref, kcmp_ref, vct_ref, ksa_ref, vst_ref, kw_ref, vwt_ref,
                   o_ref, qa_ref, sc_ref, ps_ref, s_ref, *, n_cmp_rows):
    t_idx = pl.program_id(2)
    t0 = t_idx * TQ
    width = HPG * TQ
    q4 = jnp.concatenate([qt_ref[0, h] for h in range(HPG)], axis=1)
    zeros = jnp.zeros((HEAD_DIM, width), BF16)
    q4p = jnp.concatenate([q4, zeros], axis=0)
    tq = t0 + lax.broadcasted_iota(jnp.int32, (1, width), 1) % TQ
    tq1 = t0 + lax.broadcasted_iota(jnp.int32, (1, TQ), 1)

    row_w = lax.broadcasted_iota(jnp.int32, (WIN_CHUNK, 1), 0)
    n_win = WINDOW // WIN_CHUNK + 1
    sw, vw = [], []
    for w in range(n_win):
        c_true = t_idx - (n_win - 1) + w
        c_load = jnp.maximum(c_true, 0)
        s = _dot(kw_ref[0, 0, pl.ds(pl.multiple_of(c_load * WIN_CHUNK, WIN_CHUNK), WIN_CHUNK), :], q4p)
        key = c_true * WIN_CHUNK + row_w
        if w == 0:
            s = jnp.where(key > jnp.maximum(tq - WINDOW, -1), s, NEG)
        elif w == n_win - 1:
            s = jnp.where(key <= tq, s, NEG)
        else:
            s = s + jnp.where(c_true >= 0, 0.0, NEG)
        sw.append(s)
        vw.append(vwt_ref[0, 0, c_load])
    m = sw[0].max(axis=0, keepdims=True)
    for w in range(1, n_win):
        m = jnp.maximum(m, sw[w].max(axis=0, keepdims=True))
    p = jnp.concatenate([jnp.exp2(s - m).astype(BF16) for s in sw], axis=0)
    acc = _dot(jnp.concatenate(vw, axis=1), p)
    o_win = acc[:HEAD_DIM] * (1.0 / acc[HEAD_DIM:HEAD_DIM + 1])

    per = SLC_BLOCK // CMP_STRIDE
    cur = tq1 // SLC_BLOCK
    grp = lax.broadcasted_iota(jnp.int32, (LANES, TQ), 0)
    grp_bias = jnp.where(grp <= t_idx, 0.0, NEG).astype(BF16)
    qc = jnp.concatenate([jnp.concatenate([grp_bias] * HPG, axis=1), q4, zeros], axis=0)
    edge0 = jnp.maximum(t_idx - 1, 0) * CMP_ROWGRP
    edge = pl.ds(pl.multiple_of(edge0, CMP_ROWGRP), 2 * CMP_ROWGRP)
    n_edge = edge0 + lax.broadcasted_iota(jnp.int32, (2 * CMP_ROWGRP, 1), 0)

    def cmp_select(rows):
        def run():
            sc_ref[0:rows, :] = _dot(kcmp_ref[0, 0, 0:rows, :], qc)
            sc_ref[edge, :] = jnp.where(n_edge * CMP_STRIDE + (CMP_BLOCK - 1) <= tq, sc_ref[edge, :], NEG)
            s = sc_ref[0:rows, :]
            m = jnp.max(s, axis=0, keepdims=True)
            p = jnp.exp2(s - m)
            acc = _dot(vct_ref[0, 0, :, 0:rows], p.astype(BF16))
            inv = jnp.where(m > 0.5 * NEG, 1.0 / acc[HEAD_DIM:HEAD_DIM + 1], 0.0)
            pn = p * inv
            ps = pn[:, 0:TQ]
            for h in range(1, HPG):
                ps = ps + pn[:, h * TQ:(h + 1) * TQ]
            ps_ref[0:rows, :] = ps
            nb = rows // per
            r = [ps_ref[pl.ds(k, nb, stride=per), :] for k in range(per)]
            first_row = lax.broadcasted_iota(jnp.int32, (nb, 1), 0) == 0
            imp = jnp.where(first_row, 0.0, pltpu.roll(r[per - 1], 1, 0))
            for k in range(per):
                imp = imp + r[k]
            bias = _top_blocks(imp, cur, nb)
            if nb < N_SLC_PAD:
                bias = jnp.concatenate([bias, jnp.full((N_SLC_PAD - nb, TQ), NEG, F32)], axis=0)
            bias = bias.astype(BF16)
            for half in range(2):
                bh = bias[half * SLC_HALF:(half + 1) * SLC_HALF]
                qa_ref[half] = jnp.concatenate([jnp.concatenate([bh] * HPG, axis=1), q4, zeros], axis=0)
            return acc[:HEAD_DIM] * inv
        return run

    tiles_per_variant = n_cmp_rows // (CMP_VARIANTS * CMP_ROWGRP)
    o_cmp = lax.switch(t_idx // tiles_per_variant,
                       [cmp_select((k + 1) * n_cmp_rows // CMP_VARIANTS) for k in range(CMP_VARIANTS)])

    chunks_per_half = SLC_HALF * SLC_BLOCK // SLC_CHUNK

    def slc_scores(c, slot):
        keys = ksa_ref[0, 0, pl.ds(pl.multiple_of(c * SLC_CHUNK, SLC_CHUNK), SLC_CHUNK), :]
        s_ref[slot] = _dot(keys, qa_ref[c // chunks_per_half])

    def slc_update(c, slot, carry, masked):
        m_i, acc = carry
        sc = s_ref[slot]
        if masked:
            key = c * SLC_CHUNK + lax.broadcasted_iota(jnp.int32, (SLC_CHUNK, 1), 0)
            sc = jnp.where(key <= tq, sc, NEG)
        m_new = jnp.maximum(m_i, jnp.max(sc, axis=0, keepdims=True))
        p = jnp.exp2(sc - m_new).astype(BF16)
        return m_new, jnp.exp2(m_i - m_new) * acc + _dot(vst_ref[0, 0, c], p)

    def slc_pair(i, carry):
        slc_scores(2 * i + 1, 1)
        carry = slc_update(2 * i, 0, carry, masked=False)
        slc_scores(2 * i + 2, 0)
        return slc_update(2 * i + 1, 1, carry, masked=False)

    i_diag = t0 // (2 * SLC_CHUNK)
    d0 = 2 * i_diag
    slc_scores(0, 0)
    carry = lax.fori_loop(0, i_diag, slc_pair,
                          (jnp.full((1, width), NEG, F32), jnp.zeros((V_ROWS, width), F32)))
    slc_scores(d0 + 1, 1)
    carry = slc_update(d0, 0, carry, masked=True)
    _, acc = slc_update(d0 + 1, 1, carry, masked=True)
    o_slc = acc[:HEAD_DIM] * (1.0 / acc[HEAD_DIM:HEAD_DIM + 1])

    gates = gt_ref[0]
    outs = []
    for h in range(HPG):
        cols = slice(h * TQ, (h + 1) * TQ)
        outs.append(gates[3 * h:3 * h + 1] * o_cmp[:, cols]
                    + gates[3 * h + 1:3 * h + 2] * o_slc[:, cols]
                    + gates[3 * h + 2:3 * h + 3] * o_win[:, cols])
    o_ref[0] = jnp.concatenate(outs, axis=0).T


def _nsa_attn(qt, gt, kcmp, vct, ks, vst, kw, vwt):
    b_, _, _, s_ = qt.shape
    n_cmp_rows = kcmp.shape[2]
    assert s_ % (2 * SLC_CHUNK) == 0 and (SLC_HALF * SLC_BLOCK) % SLC_CHUNK == 0 and s_ // SLC_BLOCK <= N_SLC_PAD and n_cmp_rows <= CMP_ROWGRP * LANES
    assert TQ == WIN_CHUNK and WINDOW % WIN_CHUNK == 0
    per_group = lambda *shape: _resident(
        (1, 1) + shape, lambda b, g, t: (b, g) + (0,) * len(shape))
    return pl.pallas_call(
        functools.partial(_nsa_attn_body, n_cmp_rows=n_cmp_rows),
        grid=(b_, N_KV, s_ // TQ),
        in_specs=[
            pl.BlockSpec((1, HPG, HEAD_DIM, TQ), lambda b, g, t: (b, g, 0, t)),
            pl.BlockSpec((1, GATE_PAD, TQ), lambda b, g, t: (b, g, t)),
            per_group(n_cmp_rows, 2 * LANES),
            per_group(V_ROWS, n_cmp_rows),
            per_group(s_, 2 * LANES),
            per_group(s_ // SLC_CHUNK, V_ROWS, SLC_CHUNK),
            per_group(s_, LANES),
            per_group(s_ // WIN_CHUNK, V_ROWS, WIN_CHUNK),
        ],
        out_specs=pl.BlockSpec((1, TQ, HPG * HEAD_DIM), lambda b, g, t: (b, t, g)),
        out_shape=jax.ShapeDtypeStruct((b_, s_, N_HEADS * HEAD_DIM), F32),
        scratch_shapes=[pltpu.VMEM((2, 2 * LANES, HPG * TQ), BF16),
                        pltpu.VMEM((n_cmp_rows, HPG * TQ), F32),
                        pltpu.VMEM((n_cmp_rows, TQ), F32),
                        pltpu.VMEM((2, SLC_CHUNK, HPG * TQ), F32)],
        compiler_params=pltpu.CompilerParams(
            dimension_semantics=("arbitrary", "arbitrary", "arbitrary"), vmem_limit_bytes=VMEM_LIMIT),
        name="nsa_attn",
    )(qt, gt, kcmp, vct, ks, vst, kw, vwt)


def _proj_body(o_ref, x_ref, w_ref, g_ref, b_ref, y_ref):
    y = _dot(o_ref[...].astype(BF16), w_ref[...])
    y_ref[...] = _layer_norm(ALPHA * x_ref[...] + y, g_ref[...], b_ref[...])


def _proj_ln(o2, x2, w, g, b, tm=512):
    m = x2.shape[0]
    return pl.pallas_call(
        _proj_body,
        grid=(m // tm,),
        in_specs=[
            pl.BlockSpec((tm, D_MODEL), lambda i: (i, 0)),
            pl.BlockSpec((tm, D_MODEL), lambda i: (i, 0)),
            _resident((D_MODEL, D_MODEL), lambda i: (0, 0)),
            _resident((1, D_MODEL), lambda i: (0, 0)),
            _resident((1, D_MODEL), lambda i: (0, 0)),
        ],
        out_specs=pl.BlockSpec((tm, D_MODEL), lambda i: (i, 0)),
        out_shape=jax.ShapeDtypeStruct((m, D_MODEL), F32),
        compiler_params=pltpu.CompilerParams(
            dimension_semantics=("arbitrary",), vmem_limit_bytes=VMEM_LIMIT),
        name="proj_ln",
    )(o2, x2, w.astype(BF16), g.reshape(1, -1), b.reshape(1, -1))


def _nsa_mixer_ln(x, w_in, pe_k, w1_k, w2_k, pe_v, w1_v, w2_v, w_out, g, b):
    b_, s_, _ = x.shape
    qt, gt, kc, vc, ks, vst, kw, vwt = _nsa_prep(x, w_in)
    kcmp, vct = _compress(kc, vc, pe_k, w1_k, w2_k, pe_v, w1_v, w2_v)
    o = _nsa_attn(qt, gt, kcmp, vct, ks, vst, kw, vwt)
    return _proj_ln(o.reshape(b_ * s_, -1), x.reshape(b_ * s_, -1), w_out, g, b).reshape(x.shape)


def kernel(x, l0_ffn1_w_in, l0_ffn1_w_out, l0_ln1_g, l0_ln1_b, l0_gm_w_in, l0_gm_ln_g, l0_gm_ln_b, l0_gm_w_s, l0_gm_b_s, l0_gm_w_out, l0_ln2_g, l0_ln2_b, l0_ffn2_w_in, l0_ffn2_w_out, l0_ln3_g, l0_ln3_b, l1_ffn1_w_in, l1_ffn1_w_out, l1_ln1_g, l1_ln1_b, l1_nsa_w_in, l1_nsa_cmp_pe_k, l1_nsa_cmp_w1_k, l1_nsa_cmp_w2_k, l1_nsa_cmp_pe_v, l1_nsa_cmp_w1_v, l1_nsa_cmp_w2_v, l1_nsa_w_out, l1_ln2_g, l1_ln2_b, l1_ffn2_w_in, l1_ffn2_w_out, l1_ln3_g, l1_ln3_b):
    b_, s_, d_ = x.shape
    h = x.reshape(b_ * s_, d_)
    h = _ffn_ln(h, l0_ffn1_w_in, l0_ffn1_w_out, l0_ln1_g, l0_ln1_b)
    h = _gmlp_ln(h, l0_gm_w_in, l0_gm_ln_g, l0_gm_ln_b, l0_gm_w_s, l0_gm_b_s, l0_gm_w_out, l0_ln2_g, l0_ln2_b)
    h = _ffn_ln(h, l0_ffn2_w_in, l0_ffn2_w_out, l0_ln3_g, l0_ln3_b)
    h = _ffn_ln(h, l1_ffn1_w_in, l1_ffn1_w_out, l1_ln1_g, l1_ln1_b)
    h = _nsa_mixer_ln(h.reshape(b_, s_, d_), l1_nsa_w_in, l1_nsa_cmp_pe_k, l1_nsa_cmp_w1_k, l1_nsa_cmp_w2_k,
                      l1_nsa_cmp_pe_v, l1_nsa_cmp_w1_v, l1_nsa_cmp_w2_v, l1_nsa_w_out, l1_ln2_g, l1_ln2_b)
    h = _ffn_ln(h.reshape(b_ * s_, d_), l1_ffn2_w_in, l1_ffn2_w_out, l1_ln3_g, l1_ln3_b)
    return h.reshape(b_, s_, d_)
```

```python
import functools

import jax
import jax.numpy as jnp
from jax import lax
from jax.experimental import pallas as pl
from jax.experimental.pallas import tpu as pltpu

F32 = jnp.float32
BF16 = jnp.bfloat16

D_MODEL = 1024
DEPTH = 2
ALPHA = (2 * DEPTH) ** 0.25
LN_EPS = 1e-5
D_FF = 2816
GM_WIDTH = 3 * D_MODEL
GM_CHUNK = 128
GM_GROUPS = 16
GM_GCH = GM_WIDTH // GM_GROUPS
HEAD_DIM = 64
N_HEADS = D_MODEL // HEAD_DIM
N_KV = 4
HPG = N_HEADS // N_KV
CMP_BLOCK = 32
CMP_STRIDE = 16
CMP_HIDDEN = 256
SLC_BLOCK = 64
SLC_TOP = 16
WINDOW = 512
ROPE_THETA = 10000.0
NEG = -1e30
KVW = N_KV * HEAD_DIM

LANES = 128
TQ = 256
SLC_CHUNK = 512
WIN_CHUNK = 128
CMP_ROWGRP = TQ // CMP_STRIDE
CMP_VARIANTS = 4
SLC_HALF = 128
N_SLC_PAD = 2 * SLC_HALF
V_ROWS = 80
LOG2E = 1.4426950408889634
VMEM_LIMIT = 56 * 1024 * 1024


def _resident(shape, index_map):
    return pl.BlockSpec(shape, index_map, pipeline_mode=pl.Buffered(1))


def _layer_norm(y, g, b):
    mu = jnp.mean(y, axis=-1, keepdims=True)
    d = y - mu
    var = jnp.mean(d * d, axis=-1, keepdims=True)
    return d * lax.rsqrt(var + LN_EPS) * g + b


def _gelu(x):
    return 0.5 * x * (1.0 + jnp.tanh(0.7978845608028654 * (x + 0.044715 * (x * x * x))))


def _dot(a, b):
    return jnp.dot(a, b, preferred_element_type=F32)


FFN_COLS = 256


def _ffn_body(*refs, mixed):
    if mixed:
        x_ref, mix_ref, wmix_ref, gmix_ref, bmix_ref, win_ref, wout_ref, g_ref, b_ref, o_ref, a_ref = refs
        x = _layer_norm(ALPHA * x_ref[...] + _dot(mix_ref[...].astype(BF16), wmix_ref[...]),
                        gmix_ref[...], bmix_ref[...])
    else:
        x_ref, win_ref, wout_ref, g_ref, b_ref, o_ref, a_ref = refs
        x = x_ref[...]
    xb = x.astype(BF16)
    for c in range(D_FF // FFN_COLS):
        lo = c * FFN_COLS
        gate = _dot(xb, win_ref[:, lo:lo + FFN_COLS])
        up = _dot(xb, win_ref[:, D_FF + lo:D_FF + lo + FFN_COLS])
        a_ref[:, lo:lo + FFN_COLS] = (gate * jax.nn.sigmoid(gate) * up).astype(BF16)
    y = _dot(a_ref[...], wout_ref[...])
    o_ref[...] = _layer_norm(ALPHA * x + 0.5 * y, g_ref[...], b_ref[...])


def _ffn_ln(x2, w_in, w_out, g, b, mix=None, tm=512):
    m = x2.shape[0]
    rows = pl.BlockSpec((tm, D_MODEL), lambda i: (i, 0))
    vec = _resident((1, D_MODEL), lambda i: (0, 0))
    mix_specs, mix_args = [], []
    if mix is not None:
        mix_out, w_mix, g_mix, b_mix = mix
        mix_specs = [rows, _resident((D_MODEL, D_MODEL), lambda i: (0, 0)), vec, vec]
        mix_args = [mix_out, w_mix.astype(BF16), g_mix.reshape(1, -1), b_mix.reshape(1, -1)]
    return pl.pallas_call(
        functools.partial(_ffn_body, mixed=mix is not None),
        grid=(m // tm,),
        in_specs=[rows] + mix_specs + [
            _resident((D_MODEL, 2 * D_FF), lambda i: (0, 0)),
            _resident((D_FF, D_MODEL), lambda i: (0, 0)),
            vec, vec,
        ],
        out_specs=rows,
        out_shape=jax.ShapeDtypeStruct((m, D_MODEL), F32),
        scratch_shapes=[pltpu.VMEM((tm, D_FF), BF16)],
        compiler_params=pltpu.CompilerParams(
            dimension_semantics=("arbitrary",), vmem_limit_bytes=VMEM_LIMIT),
        name="mix_ffn_ln" if mix is not None else "ffn_ln",
    )(x2, *mix_args, w_in.astype(BF16), w_out.astype(BF16), g.reshape(1, -1), b.reshape(1, -1))


GM_PAIR = 2 * GM_GCH
GM_COLS = 256


def _gmlp_body(x_ref, win_ref, lng_ref, lnb_ref, ws_ref, bs_ref, wout_ref, g_ref, b_ref, o_ref,
               v_ref, a_ref, *, tm):
    x = x_ref[...]
    xb = x.astype(BF16)
    for c in range(GM_WIDTH // GM_COLS):
        lo = GM_WIDTH + c * GM_COLS
        v_ref[:, c * GM_COLS:(c + 1) * GM_COLS] = _gelu(_dot(xb, win_ref[:, lo:lo + GM_COLS]))
    v_ref[...] = _layer_norm(v_ref[...], lng_ref[...], lnb_ref[...])

    row = lax.broadcasted_iota(jnp.int32, (GM_CHUNK, GM_CHUNK), 0)
    col = lax.broadcasted_iota(jnp.int32, (GM_CHUNK, GM_CHUNK), 1)
    causal = row >= col
    first = lax.broadcasted_iota(jnp.int32, (1, GM_PAIR), 1) < GM_GCH
    for p in range(GM_GROUPS // 2):
        lo = p * GM_PAIR
        w0 = jnp.where(causal, ws_ref[2 * p], 0.0).astype(BF16)
        w1 = jnp.where(causal, ws_ref[2 * p + 1], 0.0).astype(BF16)
        wcat = jnp.concatenate([w0, w1], axis=1)
        u = _gelu(_dot(xb, win_ref[:, lo:lo + GM_PAIR]))
        bias = bs_ref[:, lo:lo + GM_PAIR]
        for n in range(tm // GM_CHUNK):
            r0, r1 = n * GM_CHUNK, (n + 1) * GM_CHUNK
            blk = v_ref[r0:r1, lo:lo + GM_PAIR]
            rhs = jnp.concatenate([jnp.where(first, blk, 0.0).astype(BF16),
                                   jnp.where(first, 0.0, blk).astype(BF16)], axis=0)
            sp = _dot(wcat, rhs) + bias
            a_ref[r0:r1, lo:lo + GM_PAIR] = (u[r0:r1] * sp).astype(BF16)
    y = _dot(a_ref[...], wout_ref[...])
    o_ref[...] = _layer_norm(ALPHA * x + y, g_ref[...], b_ref[...])


def _gmlp_ln(x2, w_in, ln_g, ln_b, w_s, b_s, w_out, g, b, tm=256):
    m = x2.shape[0]
    bias = jnp.repeat(b_s.T, GM_GCH, axis=1)
    const = lambda *shape: _resident(shape, lambda i: (0,) * len(shape))
    return pl.pallas_call(
        functools.partial(_gmlp_body, tm=tm),
        grid=(m // tm,),
        in_specs=[
            pl.BlockSpec((tm, D_MODEL), lambda i: (i, 0)),
            const(D_MODEL, 2 * GM_WIDTH),
            const(1, GM_WIDTH), const(1, GM_WIDTH),
            const(GM_GROUPS, GM_CHUNK, GM_CHUNK),
            const(GM_CHUNK, GM_WIDTH),
            const(GM_WIDTH, D_MODEL),
            const(1, D_MODEL), const(1, D_MODEL),
        ],
        out_specs=pl.BlockSpec((tm, D_MODEL), lambda i: (i, 0)),
        out_shape=jax.ShapeDtypeStruct((m, D_MODEL), F32),
        scratch_shapes=[pltpu.VMEM((tm, GM_WIDTH), F32), pltpu.VMEM((tm, GM_WIDTH), BF16)],
        compiler_params=pltpu.CompilerParams(
            dimension_semantics=("arbitrary",), vmem_limit_bytes=VMEM_LIMIT),
        name="gmlp_ln",
    )(x2, w_in.astype(BF16), ln_g.reshape(1, -1), ln_b.reshape(1, -1), w_s, bias,
      w_out.astype(BF16), g.reshape(1, -1), b.reshape(1, -1))


GATE_PAD = 16


def _rope_tables(pos):
    half = HEAD_DIM // 2
    freq = ROPE_THETA ** (-jnp.arange(half, dtype=F32) / half)
    ang = pos.astype(F32)[:, None] * freq
    cos, sin = jnp.cos(ang), jnp.sin(ang)
    return jnp.tile(cos, (1, 4)), jnp.tile(jnp.concatenate([-sin, sin], axis=1), (1, 2))


def _rope2(y, cos, sin):
    lane = lax.broadcasted_iota(jnp.int32, (1, LANES), 1)
    first = (lane % HEAD_DIM) < (HEAD_DIM // 2)
    partner = jnp.where(first, pltpu.roll(y, LANES - HEAD_DIM // 2, 1), pltpu.roll(y, HEAD_DIM // 2, 1))
    return y * cos + partner * sin


def _nsa_prep_body(x_ref, wq_ref, wkv_ref, wg_ref, cos_ref, sin_ref,
                   qt_ref, gt_ref, kc_ref, vc_ref, ks_ref, vst_ref, kw_ref, vwt_ref, *, tm):
    xb = x_ref[0].astype(BF16)
    cos, sin = cos_ref[...], sin_ref[...]
    lane = lax.broadcasted_iota(jnp.int32, (1, LANES), 1)
    low = lane < HEAD_DIM
    scale = HEAD_DIM ** -0.5 * LOG2E

    yq = _dot(xb, wq_ref[...])
    for i in range(D_MODEL // LANES):
        qtt = (_rope2(yq[:, i * LANES:(i + 1) * LANES], cos, sin) * scale).T
        qt_ref[0, 2 * i] = qtt[:HEAD_DIM].astype(BF16)
        qt_ref[0, 2 * i + 1] = qtt[HEAD_DIM:].astype(BF16)

    ykv = _dot(xb, wkv_ref[...])
    tile = lambda part, i: ykv[:, part * KVW + i * LANES:part * KVW + (i + 1) * LANES]
    for part, ref in ((0, kc_ref), (1, vc_ref)):
        for i in range(KVW // LANES):
            y = tile(part, i)
            ref[0, 2 * i] = y[:, :HEAD_DIM].astype(BF16)
            ref[0, 2 * i + 1] = pltpu.roll(y, HEAD_DIM, 1)[:, :HEAD_DIM].astype(BF16)
    pos = pl.program_id(1) * tm + lax.broadcasted_iota(jnp.int32, (tm, 1), 0)
    onehot = jnp.where((pos // SLC_BLOCK) % SLC_HALF == lane, 1.0, 0.0).astype(BF16)
    for part, ref in ((2, ks_ref), (4, kw_ref)):
        for i in range(KVW // LANES):
            y = _rope2(tile(part, i), cos, sin)
            for j, yj in enumerate((y, pltpu.roll(y, HEAD_DIM, 1))):
                kj = jnp.where(low, yj, 0.0).astype(BF16)
                ref[0, 2 * i + j] = jnp.concatenate([onehot, kj], axis=1) if ref is ks_ref else kj
    ones_rows = jnp.where(lax.broadcasted_iota(jnp.int32, (V_ROWS - HEAD_DIM, tm), 0) == 0, 1.0, 0.0)
    for part, ref, ck in ((3, vst_ref, SLC_CHUNK), (5, vwt_ref, WIN_CHUNK)):
        for i in range(KVW // LANES):
            vt = tile(part, i).T
            for j in range(2):
                blk = jnp.concatenate([vt[j * HEAD_DIM:(j + 1) * HEAD_DIM], ones_rows], axis=0).astype(BF16)
                for c in range(tm // ck):
                    ref[0, 2 * i + j, c] = blk[:, c * ck:(c + 1) * ck]

    sg = jax.nn.sigmoid(_dot(xb, wg_ref[...]))
    gt_ref[0] = sg.T[:N_KV * GATE_PAD]


def _nsa_prep(x, w_in, tm=512):
    b_, s_, _ = x.shape
    wq = w_in[:, :D_MODEL].astype(BF16)
    wkv = w_in[:, D_MODEL:D_MODEL + 6 * KVW].astype(BF16)
    wg = w_in[:, D_MODEL + 6 * KVW:].reshape(D_MODEL, N_KV, HPG * 3)
    wg = jnp.pad(wg, ((0, 0), (0, 0), (0, GATE_PAD - HPG * 3))).reshape(D_MODEL, N_KV * GATE_PAD)
    wg = jnp.pad(wg, ((0, 0), (0, LANES - N_KV * GATE_PAD))).astype(BF16)
    cos, sin = _rope_tables(jnp.arange(s_))
    const = lambda *shape: _resident(shape, lambda b, i: (0,) * len(shape))
    out_shapes = (
        jax.ShapeDtypeStruct((b_, N_HEADS, HEAD_DIM, s_), BF16),
        jax.ShapeDtypeStruct((b_, N_KV * GATE_PAD, s_), F32),
        jax.ShapeDtypeStruct((b_, N_KV, s_, HEAD_DIM), BF16),
        jax.ShapeDtypeStruct((b_, N_KV, s_, HEAD_DIM), BF16),
        jax.ShapeDtypeStruct((b_, N_KV, s_, 2 * LANES), BF16),
        jax.ShapeDtypeStruct((b_, N_KV, s_ // SLC_CHUNK, V_ROWS, SLC_CHUNK), BF16),
        jax.ShapeDtypeStruct((b_, N_KV, s_, LANES), BF16),
        jax.ShapeDtypeStruct((b_, N_KV, s_ // WIN_CHUNK, V_ROWS, WIN_CHUNK), BF16),
    )
    out_specs = (
        pl.BlockSpec((1, N_HEADS, HEAD_DIM, tm), lambda b, i: (b, 0, 0, i)),
        pl.BlockSpec((1, N_KV * GATE_PAD, tm), lambda b, i: (b, 0, i)),
        pl.BlockSpec((1, N_KV, tm, HEAD_DIM), lambda b, i: (b, 0, i, 0)),
        pl.BlockSpec((1, N_KV, tm, HEAD_DIM), lambda b, i: (b, 0, i, 0)),
        pl.BlockSpec((1, N_KV, tm, 2 * LANES), lambda b, i: (b, 0, i, 0)),
        pl.BlockSpec((1, N_KV, tm // SLC_CHUNK, V_ROWS, SLC_CHUNK), lambda b, i: (b, 0, i, 0, 0)),
        pl.BlockSpec((1, N_KV, tm, LANES), lambda b, i: (b, 0, i, 0)),
        pl.BlockSpec((1, N_KV, tm // WIN_CHUNK, V_ROWS, WIN_CHUNK), lambda b, i: (b, 0, i, 0, 0)),
    )
    return pl.pallas_call(
        functools.partial(_nsa_prep_body, tm=tm),
        grid=(b_, s_ // tm),
        in_specs=[
            pl.BlockSpec((1, tm, D_MODEL), lambda b, i: (b, i, 0)),
            const(D_MODEL, D_MODEL), const(D_MODEL, 6 * KVW), const(D_MODEL, LANES),
            pl.BlockSpec((tm, LANES), lambda b, i: (i, 0)),
            pl.BlockSpec((tm, LANES), lambda b, i: (i, 0)),
        ],
        out_specs=out_specs,
        out_shape=out_shapes,
        compiler_params=pltpu.CompilerParams(
            dimension_semantics=("arbitrary", "arbitrary"), vmem_limit_bytes=VMEM_LIMIT),
        name="nsa_prep",
    )(x, wq, wkv, wg, cos, sin)


def _compress_body(ck_ref, cv_ref, pek_ref, pev_ref, w1k_ref, w1v_ref, w2k_ref, w2v_ref, cos_ref, sin_ref,
                   kcmp_ref, vct_ref, *, n_rows):
    half = CMP_STRIDE * HEAD_DIM

    def mlp(c_ref, pe_ref, w1_ref, w2_ref):
        c = c_ref[0, 0]
        a = _dot(c, w1_ref[:half])
        bnext = pltpu.roll(_dot(c, w1_ref[half:]), n_rows - 1, 0)
        bias = _dot(pe_ref[...], w1_ref[...])[0:1]
        h = _gelu(a + bnext + bias)
        return _dot(h.astype(BF16), w2_ref[...])

    kc = _rope2(mlp(ck_ref, pek_ref, w1k_ref, w2k_ref), cos_ref[...], sin_ref[...]).astype(BF16)
    grp = lax.broadcasted_iota(jnp.int32, (n_rows, 1), 0) // CMP_ROWGRP
    onehot = jnp.where(grp == lax.broadcasted_iota(jnp.int32, (1, LANES), 1), 1.0, 0.0).astype(BF16)
    kcmp_ref[0, 0] = jnp.concatenate([onehot, kc], axis=1)
    vt = mlp(cv_ref, pev_ref, w1v_ref, w2v_ref).T
    ones_rows = jnp.where(lax.broadcasted_iota(jnp.int32, (V_ROWS - HEAD_DIM, n_rows), 0) == 0, 1.0, 0.0)
    vct_ref[0, 0] = jnp.concatenate([vt[:HEAD_DIM], ones_rows], axis=0).astype(BF16)


def _compress(kc, vc, pe_k, w1_k, w2_k, pe_v, w1_v, w2_v):
    b_, g_, s_, _ = kc.shape
    n_rows = s_ // CMP_STRIDE
    feat = CMP_STRIDE * HEAD_DIM
    ck = kc.reshape(b_, g_, n_rows, feat)
    cv = vc.reshape(b_, g_, n_rows, feat)
    cos, sin = _rope_tables(CMP_STRIDE * jnp.arange(n_rows) + CMP_BLOCK - 1)
    pe = lambda p: jnp.pad(p.reshape(1, 2 * feat), ((0, 7), (0, 0))).astype(BF16)
    w1 = lambda w: w.reshape(2 * feat, CMP_HIDDEN).astype(BF16)
    w2 = lambda w: jnp.pad(w, ((0, 0), (0, LANES - HEAD_DIM))).astype(BF16)
    const = lambda *shape: _resident(shape, lambda b, g: (0,) * len(shape))
    blk = lambda *shape: pl.BlockSpec((1, 1) + shape, lambda b, g: (b, g, 0, 0))
    return pl.pallas_call(
        functools.partial(_compress_body, n_rows=n_rows),
        grid=(b_, g_),
        in_specs=[
            blk(n_rows, feat), blk(n_rows, feat),
            const(8, 2 * feat), const(8, 2 * feat),
            const(2 * feat, CMP_HIDDEN), const(2 * feat, CMP_HIDDEN),
            const(CMP_HIDDEN, LANES), const(CMP_HIDDEN, LANES),
            const(n_rows, LANES), const(n_rows, LANES),
        ],
        out_specs=(blk(n_rows, 2 * LANES), blk(V_ROWS, n_rows)),
        out_shape=(jax.ShapeDtypeStruct((b_, g_, n_rows, 2 * LANES), BF16),
                   jax.ShapeDtypeStruct((b_, g_, V_ROWS, n_rows), BF16)),
        compiler_params=pltpu.CompilerParams(
            dimension_semantics=("arbitrary", "arbitrary"), vmem_limit_bytes=VMEM_LIMIT),
        name="nsa_compress",
    )(ck, cv, pe(pe_k), pe(pe_v), w1(w1_k), w1(w1_v), w2(w2_k), w2(w2_v), cos, sin)


def _top_blocks(imp, cur, rows):
    j_i = lax.broadcasted_iota(jnp.int32, (rows, 1), 0)
    forced = (j_i == 0) | (j_i == cur) | (j_i == cur - 1)
    excluded, taken = -3e38, -2e38
    val = jnp.where(forced | (j_i > cur), excluded, imp)
    for _ in range(SLC_TOP - 3):
        vmax = jnp.max(val, axis=0, keepdims=True)
        jmin = jnp.min(jnp.where(val == vmax, j_i, rows), axis=0, keepdims=True)
        val = jnp.where(j_i == jmin, taken, val)
    return jnp.where((forced | (val == taken)) & (j_i <= cur), 0.0, NEG)


def _nsa_attn_body(qt_ref, gt_ref, kcmp_ref, vct_ref, ksa_ref, vst_ref, kw_ref, vwt_ref,
                   o_ref, qa_ref, sc_ref, ps_ref, s_ref, *, n_cmp_rows):
    t_idx = pl.program_id(2)
    t0 = t_idx * TQ
    width = HPG * TQ
    q4 = jnp.concatenate([qt_ref[0, h] for h in range(HPG)], axis=1)
    zeros = jnp.zeros((HEAD_DIM, width), BF16)
    q4p = jnp.concatenate([q4, zeros], axis=0)
    tq = t0 + lax.broadcasted_iota(jnp.int32, (1, width), 1) % TQ
    tq1 = t0 + lax.broadcasted_iota(jnp.int32, (1, TQ), 1)

    row_w = lax.broadcasted_iota(jnp.int32, (WIN_CHUNK, 1), 0)
    n_edge_w = TQ // WIN_CHUNK
    n_win = WINDOW // WIN_CHUNK + n_edge_w
    sw, vw = [], []
    for w in range(n_win):
        c_true = (t_idx + 1) * n_edge_w - n_win + w
        c_load = jnp.maximum(c_true, 0)
        s = _dot(kw_ref[0, 0, pl.ds(pl.multiple_of(c_load * WIN_CHUNK, WIN_CHUNK), WIN_CHUNK), :], q4p)
        key = c_true * WIN_CHUNK + row_w
        if w < n_edge_w:
            s = jnp.where(key > jnp.maximum(tq - WINDOW, -1), s, NEG)
        elif w >= n_win - n_edge_w:
            s = jnp.where(key <= tq, s, NEG)
        else:
            s = s + jnp.where(c_true >= 0, 0.0, NEG)
        sw.append(s)
        vw.append(vwt_ref[0, 0, c_load])
    m = sw[0].max(axis=0, keepdims=True)
    for w in range(1, n_win):
        m = jnp.maximum(m, sw[w].max(axis=0, keepdims=True))
    p = jnp.concatenate([jnp.exp2(s - m).astype(BF16) for s in sw], axis=0)
    acc = _dot(jnp.concatenate(vw, axis=1), p)
    o_win = acc[:HEAD_DIM] * (1.0 / acc[HEAD_DIM:HEAD_DIM + 1])

    per = SLC_BLOCK // CMP_STRIDE
    cur = tq1 // SLC_BLOCK
    grp = lax.broadcasted_iota(jnp.int32, (LANES, TQ), 0)
    grp_bias = jnp.where(grp <= t_idx, 0.0, NEG).astype(BF16)
    qc = jnp.concatenate([jnp.concatenate([grp_bias] * HPG, axis=1), q4, zeros], axis=0)
    edge0 = jnp.maximum(t_idx - 1, 0) * CMP_ROWGRP
    edge = pl.ds(pl.multiple_of(edge0, CMP_ROWGRP), 2 * CMP_ROWGRP)
    n_edge = edge0 + lax.broadcasted_iota(jnp.int32, (2 * CMP_ROWGRP, 1), 0)

    def cmp_select(rows):
        def run():
            sc_ref[0:rows, :] = _dot(kcmp_ref[0, 0, 0:rows, :], qc)
            sc_ref[edge, :] = jnp.where(n_edge * CMP_STRIDE + (CMP_BLOCK - 1) <= tq, sc_ref[edge, :], NEG)
            s = sc_ref[0:rows, :]
            m = jnp.max(s, axis=0, keepdims=True)
            p = jnp.exp2(s - m)
            acc = _dot(vct_ref[0, 0, :, 0:rows], p.astype(BF16))
            inv = jnp.where(m > 0.5 * NEG, 1.0 / acc[HEAD_DIM:HEAD_DIM + 1], 0.0)
            pn = p * inv
            ps = pn[:, 0:TQ]
            for h in range(1, HPG):
                ps = ps + pn[:, h * TQ:(h + 1) * TQ]
            for i in range(TQ // LANES):
                ps_ref[i, 0:rows, :] = ps[:, i * LANES:(i + 1) * LANES]
            nb = rows // per
            r = [jnp.concatenate([ps_ref[i, pl.ds(k, nb, stride=per), :] for i in range(TQ // LANES)], axis=1)
                 for k in range(per)]
            first_row = lax.broadcasted_iota(jnp.int32, (nb, 1), 0) == 0
            imp = jnp.where(first_row, 0.0, pltpu.roll(r[per - 1], 1, 0))
            for k in range(per):
                imp = imp + r[k]
            bias = _top_blocks(imp, cur, nb)
            if nb < N_SLC_PAD:
                bias = jnp.concatenate([bias, jnp.full((N_SLC_PAD - nb, TQ), NEG, F32)], axis=0)
            bias = bias.astype(BF16)
            for half in range(2):
                bh = bias[half * SLC_HALF:(half + 1) * SLC_HALF]
                qa_ref[half] = jnp.concatenate([jnp.concatenate([bh] * HPG, axis=1), q4, zeros], axis=0)
            return acc[:HEAD_DIM] * inv
        return run

    tiles_per_variant = n_cmp_rows // (CMP_VARIANTS * CMP_ROWGRP)
    o_cmp = lax.switch(t_idx // tiles_per_variant,
                       [cmp_select((k + 1) * n_cmp_rows // CMP_VARIANTS) for k in range(CMP_VARIANTS)])

    chunks_per_half = SLC_HALF * SLC_BLOCK // SLC_CHUNK

    def slc_scores(c, slot):
        keys = ksa_ref[0, 0, pl.ds(pl.multiple_of(c * SLC_CHUNK, SLC_CHUNK), SLC_CHUNK), :]
        sc = _dot(keys, qa_ref[c // chunks_per_half])
        s_ref[slot] = sc
        return jnp.max(sc, axis=0, keepdims=True)

    def slc_update(c, slot, m_i, acc, smax=None):
        sc = s_ref[slot]
        if smax is None:
            key = c * SLC_CHUNK + lax.broadcasted_iota(jnp.int32, (SLC_CHUNK, 1), 0)
            sc = jnp.where(key <= tq, sc, NEG)
            smax = jnp.max(sc, axis=0, keepdims=True)
        m_new = jnp.maximum(m_i, smax)
        p = jnp.exp2(sc - m_new).astype(BF16)
        return m_new, jnp.exp2(m_i - m_new) * acc + _dot(vst_ref[0, 0, c], p)

    def slc_pair(i, carry):
        m_i, acc, smax_a = carry
        smax_b = slc_scores(2 * i + 1, 1)
        m_i, acc = slc_update(2 * i, 0, m_i, acc, smax_a)
        smax_a = slc_scores(2 * i + 2, 0)
        m_i, acc = slc_update(2 * i + 1, 1, m_i, acc, smax_b)
        return m_i, acc, smax_a

    i_diag = t0 // (2 * SLC_CHUNK)
    d0 = 2 * i_diag
    smax = slc_scores(0, 0)
    m_i, acc, _ = lax.fori_loop(
        0, i_diag, slc_pair,
        (jnp.full((1, width), NEG, F32), jnp.zeros((V_ROWS, width), F32), smax))
    slc_scores(d0 + 1, 1)
    m_i, acc = slc_update(d0, 0, m_i, acc)
    _, acc = slc_update(d0 + 1, 1, m_i, acc)
    o_slc = acc[:HEAD_DIM] * (1.0 / acc[HEAD_DIM:HEAD_DIM + 1])

    gates = gt_ref[0]
    outs = []
    for h in range(HPG):
        cols = slice(h * TQ, (h + 1) * TQ)
        outs.append(gates[3 * h:3 * h + 1] * o_cmp[:, cols]
                    + gates[3 * h + 1:3 * h + 2] * o_slc[:, cols]
                    + gates[3 * h + 2:3 * h + 3] * o_win[:, cols])
    o_ref[0] = jnp.concatenate(outs, axis=0).T


def _nsa_attn(qt, gt, kcmp, vct, ks, vst, kw, vwt):
    b_, _, _, s_ = qt.shape
    n_cmp_rows = kcmp.shape[2]
    assert s_ % (2 * SLC_CHUNK) == 0 and (SLC_HALF * SLC_BLOCK) % SLC_CHUNK == 0 and s_ // SLC_BLOCK <= N_SLC_PAD and n_cmp_rows <= CMP_ROWGRP * LANES
    assert TQ % WIN_CHUNK == 0 and WINDOW % WIN_CHUNK == 0 and WINDOW >= 2 * TQ and SLC_CHUNK % TQ == 0
    per_group = lambda *shape: _resident(
        (1, 1) + shape, lambda b, g, t: (b, g) + (0,) * len(shape))
    return pl.pallas_call(
        functools.partial(_nsa_attn_body, n_cmp_rows=n_cmp_rows),
        grid=(b_, N_KV, s_ // TQ),
        in_specs=[
            pl.BlockSpec((1, HPG, HEAD_DIM, TQ), lambda b, g, t: (b, g, 0, t)),
            pl.BlockSpec((1, GATE_PAD, TQ), lambda b, g, t: (b, g, t)),
            per_group(n_cmp_rows, 2 * LANES),
            per_group(V_ROWS, n_cmp_rows),
            per_group(s_, 2 * LANES),
            per_group(s_ // SLC_CHUNK, V_ROWS, SLC_CHUNK),
            per_group(s_, LANES),
            per_group(s_ // WIN_CHUNK, V_ROWS, WIN_CHUNK),
        ],
        out_specs=pl.BlockSpec((1, TQ, HPG * HEAD_DIM), lambda b, g, t: (b, t, g)),
        out_shape=jax.ShapeDtypeStruct((b_, s_, N_HEADS * HEAD_DIM), F32),
        scratch_shapes=[pltpu.VMEM((2, 2 * LANES, HPG * TQ), BF16),
                        pltpu.VMEM((n_cmp_rows, HPG * TQ), F32),
                        pltpu.VMEM((TQ // LANES, n_cmp_rows, LANES), F32),
                        pltpu.VMEM((2, SLC_CHUNK, HPG * TQ), F32)],
        compiler_params=pltpu.CompilerParams(
            dimension_semantics=("arbitrary", "arbitrary", "arbitrary"), vmem_limit_bytes=VMEM_LIMIT),
        name="nsa_attn",
    )(qt, gt, kcmp, vct, ks, vst, kw, vwt)


def _nsa_heads(x, w_in, pe_k, w1_k, w2_k, pe_v, w1_v, w2_v):
    qt, gt, kc, vc, ks, vst, kw, vwt = _nsa_prep(x, w_in)
    kcmp, vct = _compress(kc, vc, pe_k, w1_k, w2_k, pe_v, w1_v, w2_v)
    return _nsa_attn(qt, gt, kcmp, vct, ks, vst, kw, vwt)


def kernel(x, l0_ffn1_w_in, l0_ffn1_w_out, l0_ln1_g, l0_ln1_b, l0_gm_w_in, l0_gm_ln_g, l0_gm_ln_b, l0_gm_w_s, l0_gm_b_s, l0_gm_w_out, l0_ln2_g, l0_ln2_b, l0_ffn2_w_in, l0_ffn2_w_out, l0_ln3_g, l0_ln3_b, l1_ffn1_w_in, l1_ffn1_w_out, l1_ln1_g, l1_ln1_b, l1_nsa_w_in, l1_nsa_cmp_pe_k, l1_nsa_cmp_w1_k, l1_nsa_cmp_w2_k, l1_nsa_cmp_pe_v, l1_nsa_cmp_w1_v, l1_nsa_cmp_w2_v, l1_nsa_w_out, l1_ln2_g, l1_ln2_b, l1_ffn2_w_in, l1_ffn2_w_out, l1_ln3_g, l1_ln3_b):
    b_, s_, d_ = x.shape
    h = x.reshape(b_ * s_, d_)
    h = _ffn_ln(h, l0_ffn1_w_in, l0_ffn1_w_out, l0_ln1_g, l0_ln1_b)
    h = _gmlp_ln(h, l0_gm_w_in, l0_gm_ln_g, l0_gm_ln_b, l0_gm_w_s, l0_gm_b_s, l0_gm_w_out, l0_ln2_g, l0_ln2_b)
    h = _ffn_ln(h, l0_ffn2_w_in, l0_ffn2_w_out, l0_ln3_g, l0_ln3_b)
    h = _ffn_ln(h, l1_ffn1_w_in, l1_ffn1_w_out, l1_ln1_g, l1_ln1_b)
    heads = _nsa_heads(h.reshape(b_, s_, d_), l1_nsa_w_in, l1_nsa_cmp_pe_k, l1_nsa_cmp_w1_k, l1_nsa_cmp_w2_k,
                       l1_nsa_cmp_pe_v, l1_nsa_cmp_w1_v, l1_nsa_cmp_w2_v)
    h = _ffn_ln(h, l1_ffn2_w_in, l1_ffn2_w_out, l1_ln3_g, l1_ln3_b,
                mix=(heads.reshape(b_ * s_, -1), l1_nsa_w_out, l1_ln2_g, l1_ln2_b))
    return h.reshape(b_, s_, d_)
```

```python
import functools

import jax
import jax.numpy as jnp
from jax import lax
from jax.experimental import pallas as pl
from jax.experimental.pallas import tpu as pltpu

F32 = jnp.float32
BF16 = jnp.bfloat16

D_MODEL = 1024
DEPTH = 2
ALPHA = (2 * DEPTH) ** 0.25
LN_EPS = 1e-5
D_FF = 2816
GM_WIDTH = 3 * D_MODEL
GM_CHUNK = 128
GM_GROUPS = 16
GM_GCH = GM_WIDTH // GM_GROUPS
HEAD_DIM = 64
N_HEADS = D_MODEL // HEAD_DIM
N_KV = 4
HPG = N_HEADS // N_KV
CMP_BLOCK = 32
CMP_STRIDE = 16
CMP_HIDDEN = 256
SLC_BLOCK = 64
SLC_TOP = 16
WINDOW = 512
ROPE_THETA = 10000.0
NEG = -1e30
KVW = N_KV * HEAD_DIM

LANES = 128
TQ = 256
SLC_CHUNK = 512
WIN_CHUNK = 128
CMP_ROWGRP = TQ // CMP_STRIDE
CMP_VARIANTS = 4
SLC_HALF = 128
N_SLC_PAD = 2 * SLC_HALF
V_ROWS = 80
LOG2E = 1.4426950408889634
VMEM_LIMIT = 56 * 1024 * 1024


def _resident(shape, index_map):
    return pl.BlockSpec(shape, index_map, pipeline_mode=pl.Buffered(1))


def _layer_norm(y, g, b):
    mu = jnp.mean(y, axis=-1, keepdims=True)
    d = y - mu
    var = jnp.mean(d * d, axis=-1, keepdims=True)
    return d * lax.rsqrt(var + LN_EPS) * g + b


def _gelu(x):
    return 0.5 * x * (1.0 + jnp.tanh(0.7978845608028654 * (x + 0.044715 * (x * x * x))))


def _dot(a, b):
    return jnp.dot(a, b, preferred_element_type=F32)


FFN_COLS = 256


def _ffn_body(*refs, mixed):
    if mixed:
        x_ref, mix_ref, wmix_ref, gmix_ref, bmix_ref, win_ref, wout_ref, g_ref, b_ref, o_ref, a_ref = refs
        x = _layer_norm(ALPHA * x_ref[...] + _dot(mix_ref[...].astype(BF16), wmix_ref[...]),
                        gmix_ref[...], bmix_ref[...])
    else:
        x_ref, win_ref, wout_ref, g_ref, b_ref, o_ref, a_ref = refs
        x = x_ref[...]
    xb = x.astype(BF16)
    for c in range(D_FF // FFN_COLS):
        lo = c * FFN_COLS
        gate = _dot(xb, win_ref[:, lo:lo + FFN_COLS])
        up = _dot(xb, win_ref[:, D_FF + lo:D_FF + lo + FFN_COLS])
        a_ref[:, lo:lo + FFN_COLS] = (gate * jax.nn.sigmoid(gate) * up).astype(BF16)
    y = _dot(a_ref[...], wout_ref[...])
    o_ref[...] = _layer_norm(ALPHA * x + 0.5 * y, g_ref[...], b_ref[...])


def _ffn_ln(x2, w_in, w_out, g, b, mix=None, tm=512):
    m = x2.shape[0]
    rows = pl.BlockSpec((tm, D_MODEL), lambda i: (i, 0))
    vec = _resident((1, D_MODEL), lambda i: (0, 0))
    mix_specs, mix_args = [], []
    if mix is not None:
        mix_out, w_mix, g_mix, b_mix = mix
        mix_specs = [rows, _resident((D_MODEL, D_MODEL), lambda i: (0, 0)), vec, vec]
        mix_args = [mix_out, w_mix.astype(BF16), g_mix.reshape(1, -1), b_mix.reshape(1, -1)]
    return pl.pallas_call(
        functools.partial(_ffn_body, mixed=mix is not None),
        grid=(m // tm,),
        in_specs=[rows] + mix_specs + [
            _resident((D_MODEL, 2 * D_FF), lambda i: (0, 0)),
            _resident((D_FF, D_MODEL), lambda i: (0, 0)),
            vec, vec,
        ],
        out_specs=rows,
        out_shape=jax.ShapeDtypeStruct((m, D_MODEL), F32),
        scratch_shapes=[pltpu.VMEM((tm, D_FF), BF16)],
        compiler_params=pltpu.CompilerParams(
            dimension_semantics=("arbitrary",), vmem_limit_bytes=VMEM_LIMIT),
        name="mix_ffn_ln" if mix is not None else "ffn_ln",
    )(x2, *mix_args, w_in.astype(BF16), w_out.astype(BF16), g.reshape(1, -1), b.reshape(1, -1))


GM_PAIR = 2 * GM_GCH
GM_COLS = 256


def _gmlp_body(x_ref, win_ref, lng_ref, lnb_ref, ws_ref, bs_ref, wout_ref, g_ref, b_ref, o_ref,
               v_ref, a_ref, *, tm):
    x = x_ref[...]
    xb = x.astype(BF16)
    for c in range(GM_WIDTH // GM_COLS):
        lo = GM_WIDTH + c * GM_COLS
        v_ref[:, c * GM_COLS:(c + 1) * GM_COLS] = _gelu(_dot(xb, win_ref[:, lo:lo + GM_COLS]))
    v_ref[...] = _layer_norm(v_ref[...], lng_ref[...], lnb_ref[...])

    row = lax.broadcasted_iota(jnp.int32, (GM_CHUNK, GM_CHUNK), 0)
    col = lax.broadcasted_iota(jnp.int32, (GM_CHUNK, GM_CHUNK), 1)
    causal = row >= col
    first = lax.broadcasted_iota(jnp.int32, (1, GM_PAIR), 1) < GM_GCH
    for p in range(GM_GROUPS // 2):
        lo = p * GM_PAIR
        w0 = jnp.where(causal, ws_ref[2 * p], 0.0).astype(BF16)
        w1 = jnp.where(causal, ws_ref[2 * p + 1], 0.0).astype(BF16)
        wcat = jnp.concatenate([w0, w1], axis=1)
        if p % 2 == 0:
            u2 = _gelu(_dot(xb, win_ref[:, lo:lo + 2 * GM_PAIR]))
        u = u2[:, (p % 2) * GM_PAIR:(p % 2 + 1) * GM_PAIR]
        bias = bs_ref[:, lo:lo + GM_PAIR]
        stacked = []
        for n in range(tm // GM_CHUNK):
            blk = v_ref[n * GM_CHUNK:(n + 1) * GM_CHUNK, lo:lo + GM_PAIR]
            stacked.append(jnp.concatenate([jnp.where(first, blk, 0.0).astype(BF16),
                                            jnp.where(first, 0.0, blk).astype(BF16)], axis=0))
        sp = _dot(wcat, jnp.concatenate(stacked, axis=1))
        for n in range(tm // GM_CHUNK):
            r0, r1 = n * GM_CHUNK, (n + 1) * GM_CHUNK
            a_ref[r0:r1, lo:lo + GM_PAIR] = (u[r0:r1] * (sp[:, n * GM_PAIR:(n + 1) * GM_PAIR] + bias)).astype(BF16)
    y = _dot(a_ref[...], wout_ref[...])
    o_ref[...] = _layer_norm(ALPHA * x + y, g_ref[...], b_ref[...])


def _gmlp_ln(x2, w_in, ln_g, ln_b, w_s, b_s, w_out, g, b, tm=512):
    m = x2.shape[0]
    bias = jnp.repeat(b_s.T, GM_GCH, axis=1)
    const = lambda *shape: _resident(shape, lambda i: (0,) * len(shape))
    return pl.pallas_call(
        functools.partial(_gmlp_body, tm=tm),
        grid=(m // tm,),
        in_specs=[
            pl.BlockSpec((tm, D_MODEL), lambda i: (i, 0)),
            const(D_MODEL, 2 * GM_WIDTH),
            const(1, GM_WIDTH), const(1, GM_WIDTH),
            const(GM_GROUPS, GM_CHUNK, GM_CHUNK),
            const(GM_CHUNK, GM_WIDTH),
            const(GM_WIDTH, D_MODEL),
            const(1, D_MODEL), const(1, D_MODEL),
        ],
        out_specs=pl.BlockSpec((tm, D_MODEL), lambda i: (i, 0)),
        out_shape=jax.ShapeDtypeStruct((m, D_MODEL), F32),
        scratch_shapes=[pltpu.VMEM((tm, GM_WIDTH), F32), pltpu.VMEM((tm, GM_WIDTH), BF16)],
        compiler_params=pltpu.CompilerParams(
            dimension_semantics=("arbitrary",), vmem_limit_bytes=VMEM_LIMIT),
        name="gmlp_ln",
    )(x2, w_in.astype(BF16), ln_g.reshape(1, -1), ln_b.reshape(1, -1), w_s, bias,
      w_out.astype(BF16), g.reshape(1, -1), b.reshape(1, -1))


GATE_PAD = 16


def _rope_tables(pos):
    half = HEAD_DIM // 2
    freq = ROPE_THETA ** (-jnp.arange(half, dtype=F32) / half)
    ang = pos.astype(F32)[:, None] * freq
    cos, sin = jnp.cos(ang), jnp.sin(ang)
    return jnp.tile(cos, (1, 4)), jnp.tile(jnp.concatenate([-sin, sin], axis=1), (1, 2))


def _rope2(y, cos, sin):
    lane = lax.broadcasted_iota(jnp.int32, (1, LANES), 1)
    first = (lane % HEAD_DIM) < (HEAD_DIM // 2)
    partner = jnp.where(first, pltpu.roll(y, LANES - HEAD_DIM // 2, 1), pltpu.roll(y, HEAD_DIM // 2, 1))
    return y * cos + partner * sin


def _nsa_prep_body(x_ref, wq_ref, wkv_ref, wg_ref, cos_ref, sin_ref,
                   qt_ref, gt_ref, kc_ref, vc_ref, ks_ref, vst_ref, kw_ref, vwt_ref, st_ref, *, tm):
    xb = x_ref[0].astype(BF16)
    cos, sin = cos_ref[...], sin_ref[...]
    lane = lax.broadcasted_iota(jnp.int32, (1, LANES), 1)
    low = lane < HEAD_DIM
    scale = HEAD_DIM ** -0.5 * LOG2E

    yq = _dot(xb, wq_ref[...])
    for i in range(D_MODEL // LANES):
        qtt = (_rope2(yq[:, i * LANES:(i + 1) * LANES], cos, sin) * scale).T
        qt_ref[0, 2 * i] = qtt[:HEAD_DIM].astype(BF16)
        qt_ref[0, 2 * i + 1] = qtt[HEAD_DIM:].astype(BF16)

    ykv = _dot(xb, wkv_ref[...])
    tile = lambda part, i: ykv[:, part * KVW + i * LANES:part * KVW + (i + 1) * LANES]
    for part, ref in ((0, kc_ref), (1, vc_ref)):
        for i in range(KVW // LANES):
            st_ref[...] = tile(part, i)
            tok = [st_ref[pl.ds(l, tm // CMP_STRIDE, stride=CMP_STRIDE), :] for l in range(CMP_STRIDE)]
            for j in range(2):
                cols = []
                for c in range(CMP_STRIDE // 2):
                    a, b = tok[2 * c], tok[2 * c + 1]
                    cols.append(jnp.where(low, a, pltpu.roll(b, HEAD_DIM, 1)) if j == 0
                                else jnp.where(low, pltpu.roll(a, HEAD_DIM, 1), b))
                ref[0, 2 * i + j] = jnp.concatenate(cols, axis=1).astype(BF16)
    pos = pl.program_id(1) * tm + lax.broadcasted_iota(jnp.int32, (tm, 1), 0)
    onehot = jnp.where((pos // SLC_BLOCK) % SLC_HALF == lane, 1.0, 0.0).astype(BF16)
    for part, ref in ((2, ks_ref), (4, kw_ref)):
        for i in range(KVW // LANES):
            y = _rope2(tile(part, i), cos, sin)
            for j, yj in enumerate((y, pltpu.roll(y, HEAD_DIM, 1))):
                kj = jnp.where(low, yj, 0.0).astype(BF16)
                ref[0, 2 * i + j] = jnp.concatenate([onehot, kj], axis=1) if ref is ks_ref else kj
    ones_rows = jnp.where(lax.broadcasted_iota(jnp.int32, (V_ROWS - HEAD_DIM, tm), 0) == 0, 1.0, 0.0)
    for part, ref, ck in ((3, vst_ref, SLC_CHUNK), (5, vwt_ref, WIN_CHUNK)):
        for i in range(KVW // LANES):
            vt = tile(part, i).T
            for j in range(2):
                blk = jnp.concatenate([vt[j * HEAD_DIM:(j + 1) * HEAD_DIM], ones_rows], axis=0).astype(BF16)
                for c in range(tm // ck):
                    ref[0, 2 * i + j, c] = blk[:, c * ck:(c + 1) * ck]

    sg = jax.nn.sigmoid(_dot(xb, wg_ref[...]))
    gt_ref[0] = sg.T[:N_KV * GATE_PAD]


def _nsa_prep(x, w_in, tm=512):
    b_, s_, _ = x.shape
    wq = w_in[:, :D_MODEL].astype(BF16)
    wkv = w_in[:, D_MODEL:D_MODEL + 6 * KVW].astype(BF16)
    wg = w_in[:, D_MODEL + 6 * KVW:].reshape(D_MODEL, N_KV, HPG * 3)
    wg = jnp.pad(wg, ((0, 0), (0, 0), (0, GATE_PAD - HPG * 3))).reshape(D_MODEL, N_KV * GATE_PAD)
    wg = jnp.pad(wg, ((0, 0), (0, LANES - N_KV * GATE_PAD))).astype(BF16)
    cos, sin = _rope_tables(jnp.arange(s_))
    const = lambda *shape: _resident(shape, lambda b, i: (0,) * len(shape))
    out_shapes = (
        jax.ShapeDtypeStruct((b_, N_HEADS, HEAD_DIM, s_), BF16),
        jax.ShapeDtypeStruct((b_, N_KV * GATE_PAD, s_), F32),
        jax.ShapeDtypeStruct((b_, N_KV, s_ // CMP_STRIDE, CMP_STRIDE * HEAD_DIM), BF16),
        jax.ShapeDtypeStruct((b_, N_KV, s_ // CMP_STRIDE, CMP_STRIDE * HEAD_DIM), BF16),
        jax.ShapeDtypeStruct((b_, N_KV, s_, 2 * LANES), BF16),
        jax.ShapeDtypeStruct((b_, N_KV, s_ // SLC_CHUNK, V_ROWS, SLC_CHUNK), BF16),
        jax.ShapeDtypeStruct((b_, N_KV, s_, LANES), BF16),
        jax.ShapeDtypeStruct((b_, N_KV, s_ // WIN_CHUNK, V_ROWS, WIN_CHUNK), BF16),
    )
    out_specs = (
        pl.BlockSpec((1, N_HEADS, HEAD_DIM, tm), lambda b, i: (b, 0, 0, i)),
        pl.BlockSpec((1, N_KV * GATE_PAD, tm), lambda b, i: (b, 0, i)),
        pl.BlockSpec((1, N_KV, tm // CMP_STRIDE, CMP_STRIDE * HEAD_DIM), lambda b, i: (b, 0, i, 0)),
        pl.BlockSpec((1, N_KV, tm // CMP_STRIDE, CMP_STRIDE * HEAD_DIM), lambda b, i: (b, 0, i, 0)),
        pl.BlockSpec((1, N_KV, tm, 2 * LANES), lambda b, i: (b, 0, i, 0)),
        pl.BlockSpec((1, N_KV, tm // SLC_CHUNK, V_ROWS, SLC_CHUNK), lambda b, i: (b, 0, i, 0, 0)),
        pl.BlockSpec((1, N_KV, tm, LANES), lambda b, i: (b, 0, i, 0)),
        pl.BlockSpec((1, N_KV, tm // WIN_CHUNK, V_ROWS, WIN_CHUNK), lambda b, i: (b, 0, i, 0, 0)),
    )
    return pl.pallas_call(
        functools.partial(_nsa_prep_body, tm=tm),
        grid=(b_, s_ // tm),
        in_specs=[
            pl.BlockSpec((1, tm, D_MODEL), lambda b, i: (b, i, 0)),
            const(D_MODEL, D_MODEL), const(D_MODEL, 6 * KVW), const(D_MODEL, LANES),
            pl.BlockSpec((tm, LANES), lambda b, i: (i, 0)),
            pl.BlockSpec((tm, LANES), lambda b, i: (i, 0)),
        ],
        out_specs=out_specs,
        out_shape=out_shapes,
        scratch_shapes=[pltpu.VMEM((tm, LANES), F32)],
        compiler_params=pltpu.CompilerParams(
            dimension_semantics=("arbitrary", "arbitrary"), vmem_limit_bytes=VMEM_LIMIT),
        name="nsa_prep",
    )(x, wq, wkv, wg, cos, sin)


def _compress_body(ck_ref, cv_ref, pek_ref, pev_ref, w1k_ref, w1v_ref, w2k_ref, w2v_ref, cos_ref, sin_ref,
                   kcmp_ref, vct_ref, *, n_rows):
    half = CMP_STRIDE * HEAD_DIM

    def mlp(c_ref, pe_ref, w1_ref, w2_ref):
        c = c_ref[0, 0]
        a = _dot(c, w1_ref[:half])
        bnext = pltpu.roll(_dot(c, w1_ref[half:]), n_rows - 1, 0)
        bias = _dot(pe_ref[...], w1_ref[...])[0:1]
        h = _gelu(a + bnext + bias)
        return _dot(h.astype(BF16), w2_ref[...])

    kc = _rope2(mlp(ck_ref, pek_ref, w1k_ref, w2k_ref), cos_ref[...], sin_ref[...]).astype(BF16)
    grp = lax.broadcasted_iota(jnp.int32, (n_rows, 1), 0) // CMP_ROWGRP
    onehot = jnp.where(grp == lax.broadcasted_iota(jnp.int32, (1, LANES), 1), 1.0, 0.0).astype(BF16)
    kcmp_ref[0, 0] = jnp.concatenate([onehot, kc], axis=1)
    vt = mlp(cv_ref, pev_ref, w1v_ref, w2v_ref).T
    ones_rows = jnp.where(lax.broadcasted_iota(jnp.int32, (V_ROWS - HEAD_DIM, n_rows), 0) == 0, 1.0, 0.0)
    vct_ref[0, 0] = jnp.concatenate([vt[:HEAD_DIM], ones_rows], axis=0).astype(BF16)


def _compress(ck, cv, pe_k, w1_k, w2_k, pe_v, w1_v, w2_v):
    b_, g_, n_rows, feat = ck.shape
    cos, sin = _rope_tables(CMP_STRIDE * jnp.arange(n_rows) + CMP_BLOCK - 1)
    pe = lambda p: jnp.pad(p.reshape(1, 2 * feat), ((0, 7), (0, 0))).astype(BF16)
    w1 = lambda w: w.reshape(2 * feat, CMP_HIDDEN).astype(BF16)
    w2 = lambda w: jnp.pad(w, ((0, 0), (0, LANES - HEAD_DIM))).astype(BF16)
    const = lambda *shape: _resident(shape, lambda b, g: (0,) * len(shape))
    blk = lambda *shape: pl.BlockSpec((1, 1) + shape, lambda b, g: (b, g, 0, 0))
    return pl.pallas_call(
        functools.partial(_compress_body, n_rows=n_rows),
        grid=(b_, g_),
        in_specs=[
            blk(n_rows, feat), blk(n_rows, feat),
            const(8, 2 * feat), const(8, 2 * feat),
            const(2 * feat, CMP_HIDDEN), const(2 * feat, CMP_HIDDEN),
            const(CMP_HIDDEN, LANES), const(CMP_HIDDEN, LANES),
            const(n_rows, LANES), const(n_rows, LANES),
        ],
        out_specs=(blk(n_rows, 2 * LANES), blk(V_ROWS, n_rows)),
        out_shape=(jax.ShapeDtypeStruct((b_, g_, n_rows, 2 * LANES), BF16),
                   jax.ShapeDtypeStruct((b_, g_, V_ROWS, n_rows), BF16)),
        compiler_params=pltpu.CompilerParams(
            dimension_semantics=("arbitrary", "arbitrary"), vmem_limit_bytes=VMEM_LIMIT),
        name="nsa_compress",
    )(ck, cv, pe(pe_k), pe(pe_v), w1(w1_k), w1(w1_v), w2(w2_k), w2(w2_v), cos, sin)


def _top_blocks(imp, cur, rows):
    j_i = lax.broadcasted_iota(jnp.int32, (rows, 1), 0)
    forced = (j_i == 0) | (j_i == cur) | (j_i == cur - 1)
    excluded, taken = -3e38, -2e38
    val = jnp.where(forced | (j_i > cur), excluded, imp)
    for _ in range(SLC_TOP - 3):
        vmax = jnp.max(val, axis=0, keepdims=True)
        jmin = jnp.min(jnp.where(val == vmax, j_i, rows), axis=0, keepdims=True)
        val = jnp.where(j_i == jmin, taken, val)
    return jnp.where((forced | (val == taken)) & (j_i <= cur), 0.0, NEG)


def _nsa_attn_body(qt_ref, gt_ref, kcmp_ref, vct_ref, ksa_ref, vst_ref, kw_ref, vwt_ref,
                   o_ref, qa_ref, sc_ref, ps_ref, s_ref, *, n_cmp_rows):
    t_idx = pl.program_id(2)
    t0 = t_idx * TQ
    width = HPG * TQ
    q4 = jnp.concatenate([qt_ref[0, h] for h in range(HPG)], axis=1)
    zeros = jnp.zeros((HEAD_DIM, width), BF16)
    q4p = jnp.concatenate([q4, zeros], axis=0)
    tq = t0 + lax.broadcasted_iota(jnp.int32, (1, width), 1) % TQ
    tq1 = t0 + lax.broadcasted_iota(jnp.int32, (1, TQ), 1)

    def window_branch():
        row_w = lax.broadcasted_iota(jnp.int32, (WIN_CHUNK, 1), 0)
        n_edge_w = TQ // WIN_CHUNK
        n_win = WINDOW // WIN_CHUNK + n_edge_w
        sw, vw = [], []
        for w in range(n_win):
            c_true = (t_idx + 1) * n_edge_w - n_win + w
            c_load = jnp.maximum(c_true, 0)
            s = _dot(kw_ref[0, 0, pl.ds(pl.multiple_of(c_load * WIN_CHUNK, WIN_CHUNK), WIN_CHUNK), :], q4p)
            key = c_true * WIN_CHUNK + row_w
            if w < n_edge_w:
                s = jnp.where(key > jnp.maximum(tq - WINDOW, -1), s, NEG)
            elif w >= n_win - n_edge_w:
                s = jnp.where(key <= tq, s, NEG)
            else:
                s = s + jnp.where(c_true >= 0, 0.0, NEG)
            sw.append(s)
            vw.append(vwt_ref[0, 0, c_load])
        m = sw[0].max(axis=0, keepdims=True)
        for w in range(1, n_win):
            m = jnp.maximum(m, sw[w].max(axis=0, keepdims=True))
        p = jnp.concatenate([jnp.exp2(s - m).astype(BF16) for s in sw], axis=0)
        acc = _dot(jnp.concatenate(vw, axis=1), p)
        return acc[:HEAD_DIM] * (1.0 / acc[HEAD_DIM:HEAD_DIM + 1])

    per = SLC_BLOCK // CMP_STRIDE
    cur = tq1 // SLC_BLOCK
    grp = lax.broadcasted_iota(jnp.int32, (LANES, TQ), 0)
    grp_bias = jnp.where(grp <= t_idx, 0.0, NEG).astype(BF16)
    qc = jnp.concatenate([jnp.concatenate([grp_bias] * HPG, axis=1), q4, zeros], axis=0)
    edge0 = jnp.maximum(t_idx - 1, 0) * CMP_ROWGRP
    edge = pl.ds(pl.multiple_of(edge0, CMP_ROWGRP), 2 * CMP_ROWGRP)
    n_edge = edge0 + lax.broadcasted_iota(jnp.int32, (2 * CMP_ROWGRP, 1), 0)

    def cmp_select(rows):
        def run():
            sc_ref[0:rows, :] = _dot(kcmp_ref[0, 0, 0:rows, :], qc)
            sc_ref[edge, :] = jnp.where(n_edge * CMP_STRIDE + (CMP_BLOCK - 1) <= tq, sc_ref[edge, :], NEG)
            s = sc_ref[0:rows, :]
            m = jnp.max(s, axis=0, keepdims=True)
            p = jnp.exp2(s - m)
            acc = _dot(vct_ref[0, 0, :, 0:rows], p.astype(BF16))
            inv = jnp.where(m > 0.5 * NEG, 1.0 / acc[HEAD_DIM:HEAD_DIM + 1], 0.0)
            pn = p * inv
            ps = pn[:, 0:TQ]
            for h in range(1, HPG):
                ps = ps + pn[:, h * TQ:(h + 1) * TQ]
            for i in range(TQ // LANES):
                ps_ref[i, 0:rows, :] = ps[:, i * LANES:(i + 1) * LANES]
            nb = rows // per
            r = [jnp.concatenate([ps_ref[i, pl.ds(k, nb, stride=per), :] for i in range(TQ // LANES)], axis=1)
                 for k in range(per)]
            first_row = lax.broadcasted_iota(jnp.int32, (nb, 1), 0) == 0
            imp = jnp.where(first_row, 0.0, pltpu.roll(r[per - 1], 1, 0))
            for k in range(per):
                imp = imp + r[k]
            bias = _top_blocks(imp, cur, nb)
            if nb < N_SLC_PAD:
                bias = jnp.concatenate([bias, jnp.full((N_SLC_PAD - nb, TQ), NEG, F32)], axis=0)
            bias = bias.astype(BF16)
            for half in range(2):
                bh = bias[half * SLC_HALF:(half + 1) * SLC_HALF]
                qa_ref[half] = jnp.concatenate([jnp.concatenate([bh] * HPG, axis=1), q4, zeros], axis=0)
            return acc[:HEAD_DIM] * inv
        return run

    o_win = window_branch()
    tiles_per_variant = n_cmp_rows // (CMP_VARIANTS * CMP_ROWGRP)
    o_cmp = lax.switch(t_idx // tiles_per_variant,
                       [cmp_select((k + 1) * n_cmp_rows // CMP_VARIANTS) for k in range(CMP_VARIANTS)])

    chunks_per_half = SLC_HALF * SLC_BLOCK // SLC_CHUNK

    def slc_scores(c, slot):
        keys = ksa_ref[0, 0, pl.ds(pl.multiple_of(c * SLC_CHUNK, SLC_CHUNK), SLC_CHUNK), :]
        sc = _dot(keys, qa_ref[c // chunks_per_half])
        s_ref[slot] = sc
        return jnp.max(sc, axis=0, keepdims=True)

    def slc_update(c, slot, m_i, acc, smax=None):
        sc = s_ref[slot]
        if smax is None:
            key = c * SLC_CHUNK + lax.broadcasted_iota(jnp.int32, (SLC_CHUNK, 1), 0)
            sc = jnp.where(key <= tq, sc, NEG)
            smax = jnp.max(sc, axis=0, keepdims=True)
        m_new = jnp.maximum(m_i, smax)
        p = jnp.exp2(sc - m_new).astype(BF16)
        return m_new, jnp.exp2(m_i - m_new) * acc + _dot(vst_ref[0, 0, c], p)

    def slc_pair(i, carry):
        m_i, acc, smax_a = carry
        smax_b = slc_scores(2 * i + 1, 1)
        m_i, acc = slc_update(2 * i, 0, m_i, acc, smax_a)
        smax_a = slc_scores(2 * i + 2, 0)
        m_i, acc = slc_update(2 * i + 1, 1, m_i, acc, smax_b)
        return m_i, acc, smax_a

    i_diag = t0 // (2 * SLC_CHUNK)
    d0 = 2 * i_diag
    smax = slc_scores(0, 0)
    m_i, acc, smax = lax.fori_loop(
        0, i_diag, slc_pair,
        (jnp.full((1, width), NEG, F32), jnp.zeros((V_ROWS, width), F32), smax))

    def tile_in_first():
        return slc_update(d0, 0, m_i, acc)[1]

    def tile_in_second():
        slc_scores(d0 + 1, 1)
        m_mid, acc_mid = slc_update(d0, 0, m_i, acc, smax)
        return slc_update(d0 + 1, 1, m_mid, acc_mid)[1]

    acc = lax.cond(t0 >= (d0 + 1) * SLC_CHUNK, tile_in_second, tile_in_first)
    o_slc = acc[:HEAD_DIM] * (1.0 / acc[HEAD_DIM:HEAD_DIM + 1])

    gates = gt_ref[0]
    outs = []
    for h in range(HPG):
        cols = slice(h * TQ, (h + 1) * TQ)
        outs.append(gates[3 * h:3 * h + 1] * o_cmp[:, cols]
                    + gates[3 * h + 1:3 * h + 2] * o_slc[:, cols]
                    + gates[3 * h + 2:3 * h + 3] * o_win[:, cols])
    o_ref[0] = jnp.concatenate(outs, axis=0).T


def _nsa_attn(qt, gt, kcmp, vct, ks, vst, kw, vwt):
    b_, _, _, s_ = qt.shape
    n_cmp_rows = kcmp.shape[2]
    assert s_ % (2 * SLC_CHUNK) == 0 and (SLC_HALF * SLC_BLOCK) % SLC_CHUNK == 0 and s_ // SLC_BLOCK <= N_SLC_PAD and n_cmp_rows <= CMP_ROWGRP * LANES
    assert TQ % WIN_CHUNK == 0 and WINDOW % WIN_CHUNK == 0 and WINDOW >= TQ and SLC_CHUNK % TQ == 0
    per_group = lambda *shape: _resident(
        (1, 1) + shape, lambda b, g, t: (b, g) + (0,) * len(shape))
    return pl.pallas_call(
        functools.partial(_nsa_attn_body, n_cmp_rows=n_cmp_rows),
        grid=(b_, N_KV, s_ // TQ),
        in_specs=[
            pl.BlockSpec((1, HPG, HEAD_DIM, TQ), lambda b, g, t: (b, g, 0, t)),
            pl.BlockSpec((1, GATE_PAD, TQ), lambda b, g, t: (b, g, t)),
            per_group(n_cmp_rows, 2 * LANES),
            per_group(V_ROWS, n_cmp_rows),
            per_group(s_, 2 * LANES),
            per_group(s_ // SLC_CHUNK, V_ROWS, SLC_CHUNK),
            per_group(s_, LANES),
            per_group(s_ // WIN_CHUNK, V_ROWS, WIN_CHUNK),
        ],
        out_specs=pl.BlockSpec((1, TQ, HPG * HEAD_DIM), lambda b, g, t: (b, t, g)),
        out_shape=jax.ShapeDtypeStruct((b_, s_, N_HEADS * HEAD_DIM), F32),
        scratch_shapes=[pltpu.VMEM((2, 2 * LANES, HPG * TQ), BF16),
                        pltpu.VMEM((n_cmp_rows, HPG * TQ), F32),
                        pltpu.VMEM((TQ // LANES, n_cmp_rows, LANES), F32),
                        pltpu.VMEM((2, SLC_CHUNK, HPG * TQ), F32)],
        compiler_params=pltpu.CompilerParams(
            dimension_semantics=("arbitrary", "arbitrary", "arbitrary"), vmem_limit_bytes=VMEM_LIMIT),
        name="nsa_attn",
    )(qt, gt, kcmp, vct, ks, vst, kw, vwt)


def _nsa_heads(x, w_in, pe_k, w1_k, w2_k, pe_v, w1_v, w2_v):
    qt, gt, kc, vc, ks, vst, kw, vwt = _nsa_prep(x, w_in)
    kcmp, vct = _compress(kc, vc, pe_k, w1_k, w2_k, pe_v, w1_v, w2_v)
    return _nsa_attn(qt, gt, kcmp, vct, ks, vst, kw, vwt)


def kernel(x, l0_ffn1_w_in, l0_ffn1_w_out, l0_ln1_g, l0_ln1_b, l0_gm_w_in, l0_gm_ln_g, l0_gm_ln_b, l0_gm_w_s, l0_gm_b_s, l0_gm_w_out, l0_ln2_g, l0_ln2_b, l0_ffn2_w_in, l0_ffn2_w_out, l0_ln3_g, l0_ln3_b, l1_ffn1_w_in, l1_ffn1_w_out, l1_ln1_g, l1_ln1_b, l1_nsa_w_in, l1_nsa_cmp_pe_k, l1_nsa_cmp_w1_k, l1_nsa_cmp_w2_k, l1_nsa_cmp_pe_v, l1_nsa_cmp_w1_v, l1_nsa_cmp_w2_v, l1_nsa_w_out, l1_ln2_g, l1_ln2_b, l1_ffn2_w_in, l1_ffn2_w_out, l1_ln3_g, l1_ln3_b):
    b_, s_, d_ = x.shape
    h = x.reshape(b_ * s_, d_)
    h = _ffn_ln(h, l0_ffn1_w_in, l0_ffn1_w_out, l0_ln1_g, l0_ln1_b)
    h = _gmlp_ln(h, l0_gm_w_in, l0_gm_ln_g, l0_gm_ln_b, l0_gm_w_s, l0_gm_b_s, l0_gm_w_out, l0_ln2_g, l0_ln2_b)
    h = _ffn_ln(h, l0_ffn2_w_in, l0_ffn2_w_out, l0_ln3_g, l0_ln3_b)
    h = _ffn_ln(h, l1_ffn1_w_in, l1_ffn1_w_out, l1_ln1_g, l1_ln1_b)
    heads = _nsa_heads(h.reshape(b_, s_, d_), l1_nsa_w_in, l1_nsa_cmp_pe_k, l1_nsa_cmp_w1_k, l1_nsa_cmp_w2_k,
                       l1_nsa_cmp_pe_v, l1_nsa_cmp_w1_v, l1_nsa_cmp_w2_v)
    h = _ffn_ln(h, l1_ffn2_w_in, l1_ffn2_w_out, l1_ln3_g, l1_ln3_b,
                mix=(heads.reshape(b_ * s_, -1), l1_nsa_w_out, l1_ln2_g, l1_ln2_b))
    return h.reshape(b_, s_, d_)
```

```python
import functools

import jax
import jax.numpy as jnp
from jax import lax
from jax.experimental import pallas as pl
from jax.experimental.pallas import tpu as pltpu

F32 = jnp.float32
BF16 = jnp.bfloat16

D_MODEL = 1024
DEPTH = 2
ALPHA = (2 * DEPTH) ** 0.25
LN_EPS = 1e-5
D_FF = 2816
GM_WIDTH = 3 * D_MODEL
GM_CHUNK = 128
GM_GROUPS = 16
GM_GCH = GM_WIDTH // GM_GROUPS
HEAD_DIM = 64
N_HEADS = D_MODEL // HEAD_DIM
N_KV = 4
HPG = N_HEADS // N_KV
CMP_BLOCK = 32
CMP_STRIDE = 16
CMP_HIDDEN = 256
SLC_BLOCK = 64
SLC_TOP = 16
WINDOW = 512
ROPE_THETA = 10000.0
NEG = -1e30
KVW = N_KV * HEAD_DIM

LANES = 128
TQ = 256
SLC_CHUNK = 512
WIN_CHUNK = 128
CMP_ROWGRP = TQ // CMP_STRIDE
CMP_VARIANTS = 8
SLC_HALF = 128
N_SLC_PAD = 2 * SLC_HALF
V_ROWS = 80
LOG2E = 1.4426950408889634
VMEM_LIMIT = 56 * 1024 * 1024


def _resident(shape, index_map):
    return pl.BlockSpec(shape, index_map, pipeline_mode=pl.Buffered(1))


def _layer_norm(y, g, b):
    mu = jnp.mean(y, axis=-1, keepdims=True)
    d = y - mu
    var = jnp.mean(d * d, axis=-1, keepdims=True)
    return d * lax.rsqrt(var + LN_EPS) * g + b


def _gelu(x):
    return 0.5 * x * (1.0 + jnp.tanh(0.7978845608028654 * (x + 0.044715 * (x * x * x))))


def _dot(a, b):
    return jnp.dot(a, b, preferred_element_type=F32)


FFN_COLS = 256


def _ffn_body(*refs, mixed):
    if mixed:
        x_ref, mix_ref, wmix_ref, gmix_ref, bmix_ref, win_ref, wout_ref, g_ref, b_ref, o_ref, a_ref = refs
        x = _layer_norm(ALPHA * x_ref[...] + _dot(mix_ref[...].astype(BF16), wmix_ref[...]),
                        gmix_ref[...], bmix_ref[...])
    else:
        x_ref, win_ref, wout_ref, g_ref, b_ref, o_ref, a_ref = refs
        x = x_ref[...]
    xb = x.astype(BF16)
    for c in range(D_FF // FFN_COLS):
        lo = c * FFN_COLS
        gate = _dot(xb, win_ref[:, lo:lo + FFN_COLS])
        up = _dot(xb, win_ref[:, D_FF + lo:D_FF + lo + FFN_COLS])
        a_ref[:, lo:lo + FFN_COLS] = (gate * jax.nn.sigmoid(gate) * up).astype(BF16)
    y = _dot(a_ref[...], wout_ref[...])
    o_ref[...] = _layer_norm(ALPHA * x + 0.5 * y, g_ref[...], b_ref[...])


def _ffn_ln(x2, w_in, w_out, g, b, mix=None, tm=512):
    m = x2.shape[0]
    rows = pl.BlockSpec((tm, D_MODEL), lambda i: (i, 0))
    vec = _resident((1, D_MODEL), lambda i: (0, 0))
    mix_specs, mix_args = [], []
    if mix is not None:
        mix_out, w_mix, g_mix, b_mix = mix
        mix_specs = [rows, _resident((D_MODEL, D_MODEL), lambda i: (0, 0)), vec, vec]
        mix_args = [mix_out, w_mix.astype(BF16), g_mix.reshape(1, -1), b_mix.reshape(1, -1)]
    return pl.pallas_call(
        functools.partial(_ffn_body, mixed=mix is not None),
        grid=(m // tm,),
        in_specs=[rows] + mix_specs + [
            _resident((D_MODEL, 2 * D_FF), lambda i: (0, 0)),
            _resident((D_FF, D_MODEL), lambda i: (0, 0)),
            vec, vec,
        ],
        out_specs=rows,
        out_shape=jax.ShapeDtypeStruct((m, D_MODEL), F32),
        scratch_shapes=[pltpu.VMEM((tm, D_FF), BF16)],
        compiler_params=pltpu.CompilerParams(
            dimension_semantics=("arbitrary",), vmem_limit_bytes=VMEM_LIMIT),
        name="mix_ffn_ln" if mix is not None else "ffn_ln",
    )(x2, *mix_args, w_in.astype(BF16), w_out.astype(BF16), g.reshape(1, -1), b.reshape(1, -1))


GM_PAIR = 2 * GM_GCH
GM_COLS = 256


def _gmlp_body(x_ref, win_ref, lng_ref, lnb_ref, ws_ref, bs_ref, wout_ref, g_ref, b_ref, o_ref,
               v_ref, a_ref, *, tm):
    x = x_ref[...]
    xb = x.astype(BF16)
    for c in range(GM_WIDTH // GM_COLS):
        lo = GM_WIDTH + c * GM_COLS
        v_ref[:, c * GM_COLS:(c + 1) * GM_COLS] = _gelu(_dot(xb, win_ref[:, lo:lo + GM_COLS]))
    v_ref[...] = _layer_norm(v_ref[...], lng_ref[...], lnb_ref[...])

    row = lax.broadcasted_iota(jnp.int32, (GM_CHUNK, GM_CHUNK), 0)
    col = lax.broadcasted_iota(jnp.int32, (GM_CHUNK, GM_CHUNK), 1)
    causal = row >= col
    first = lax.broadcasted_iota(jnp.int32, (1, GM_PAIR), 1) < GM_GCH
    for p in range(GM_GROUPS // 2):
        lo = p * GM_PAIR
        w0 = jnp.where(causal, ws_ref[2 * p], 0.0).astype(BF16)
        w1 = jnp.where(causal, ws_ref[2 * p + 1], 0.0).astype(BF16)
        wcat = jnp.concatenate([w0, w1], axis=1)
        if p % 2 == 0:
            u2 = _gelu(_dot(xb, win_ref[:, lo:lo + 2 * GM_PAIR]))
        u = u2[:, (p % 2) * GM_PAIR:(p % 2 + 1) * GM_PAIR]
        bias = bs_ref[:, lo:lo + GM_PAIR]
        stacked = []
        for n in range(tm // GM_CHUNK):
            blk = v_ref[n * GM_CHUNK:(n + 1) * GM_CHUNK, lo:lo + GM_PAIR]
            stacked.append(jnp.concatenate([jnp.where(first, blk, 0.0).astype(BF16),
                                            jnp.where(first, 0.0, blk).astype(BF16)], axis=0))
        sp = _dot(wcat, jnp.concatenate(stacked, axis=1))
        for n in range(tm // GM_CHUNK):
            r0, r1 = n * GM_CHUNK, (n + 1) * GM_CHUNK
            a_ref[r0:r1, lo:lo + GM_PAIR] = (u[r0:r1] * (sp[:, n * GM_PAIR:(n + 1) * GM_PAIR] + bias)).astype(BF16)
    y = _dot(a_ref[...], wout_ref[...])
    o_ref[...] = _layer_norm(ALPHA * x + y, g_ref[...], b_ref[...])


def _gmlp_ln(x2, w_in, ln_g, ln_b, w_s, b_s, w_out, g, b, tm=512):
    m = x2.shape[0]
    bias = jnp.repeat(b_s.T, GM_GCH, axis=1)
    const = lambda *shape: _resident(shape, lambda i: (0,) * len(shape))
    return pl.pallas_call(
        functools.partial(_gmlp_body, tm=tm),
        grid=(m // tm,),
        in_specs=[
            pl.BlockSpec((tm, D_MODEL), lambda i: (i, 0)),
            const(D_MODEL, 2 * GM_WIDTH),
            const(1, GM_WIDTH), const(1, GM_WIDTH),
            const(GM_GROUPS, GM_CHUNK, GM_CHUNK),
            const(GM_CHUNK, GM_WIDTH),
            const(GM_WIDTH, D_MODEL),
            const(1, D_MODEL), const(1, D_MODEL),
        ],
        out_specs=pl.BlockSpec((tm, D_MODEL), lambda i: (i, 0)),
        out_shape=jax.ShapeDtypeStruct((m, D_MODEL), F32),
        scratch_shapes=[pltpu.VMEM((tm, GM_WIDTH), F32), pltpu.VMEM((tm, GM_WIDTH), BF16)],
        compiler_params=pltpu.CompilerParams(
            dimension_semantics=("arbitrary",), vmem_limit_bytes=VMEM_LIMIT),
        name="gmlp_ln",
    )(x2, w_in.astype(BF16), ln_g.reshape(1, -1), ln_b.reshape(1, -1), w_s, bias,
      w_out.astype(BF16), g.reshape(1, -1), b.reshape(1, -1))


GATE_PAD = 16


def _rope_tables(pos):
    half = HEAD_DIM // 2
    freq = ROPE_THETA ** (-jnp.arange(half, dtype=F32) / half)
    ang = pos.astype(F32)[:, None] * freq
    cos, sin = jnp.cos(ang), jnp.sin(ang)
    return jnp.tile(cos, (1, 4)), jnp.tile(jnp.concatenate([-sin, sin], axis=1), (1, 2))


def _rope2(y, cos, sin):
    lane = lax.broadcasted_iota(jnp.int32, (1, LANES), 1)
    first = (lane % HEAD_DIM) < (HEAD_DIM // 2)
    partner = jnp.where(first, pltpu.roll(y, LANES - HEAD_DIM // 2, 1), pltpu.roll(y, HEAD_DIM // 2, 1))
    return y * cos + partner * sin


def _nsa_prep_body(x_ref, wq_ref, wkv_ref, wg_ref, cos_ref, sin_ref,
                   qt_ref, gt_ref, kc_ref, vc_ref, ks_ref, vst_ref, kw_ref, vwt_ref, st_ref, *, tm):
    xb = x_ref[0].astype(BF16)
    cos, sin = cos_ref[...], sin_ref[...]
    lane = lax.broadcasted_iota(jnp.int32, (1, LANES), 1)
    low = lane < HEAD_DIM
    scale = HEAD_DIM ** -0.5 * LOG2E

    yq = _dot(xb, wq_ref[...])
    for i in range(D_MODEL // LANES):
        qtt = (_rope2(yq[:, i * LANES:(i + 1) * LANES], cos, sin) * scale).T
        qt_ref[0, 2 * i] = qtt[:HEAD_DIM].astype(BF16)
        qt_ref[0, 2 * i + 1] = qtt[HEAD_DIM:].astype(BF16)

    ykv = _dot(xb, wkv_ref[...])
    tile = lambda part, i: ykv[:, part * KVW + i * LANES:part * KVW + (i + 1) * LANES]
    for part, ref in ((0, kc_ref), (1, vc_ref)):
        for i in range(KVW // LANES):
            st_ref[...] = tile(part, i)
            tok = [st_ref[pl.ds(l, tm // CMP_STRIDE, stride=CMP_STRIDE), :] for l in range(CMP_STRIDE)]
            for j in range(2):
                cols = []
                for c in range(CMP_STRIDE // 2):
                    a, b = tok[2 * c], tok[2 * c + 1]
                    cols.append(jnp.where(low, a, pltpu.roll(b, HEAD_DIM, 1)) if j == 0
                                else jnp.where(low, pltpu.roll(a, HEAD_DIM, 1), b))
                ref[0, 2 * i + j] = jnp.concatenate(cols, axis=1).astype(BF16)
    pos = pl.program_id(1) * tm + lax.broadcasted_iota(jnp.int32, (tm, 1), 0)
    onehot = jnp.where((pos // SLC_BLOCK) % SLC_HALF == lane, 1.0, 0.0).astype(BF16)
    for part, ref in ((2, ks_ref), (4, kw_ref)):
        for i in range(KVW // LANES):
            y = _rope2(tile(part, i), cos, sin)
            for j, yj in enumerate((y, pltpu.roll(y, HEAD_DIM, 1))):
                kj = jnp.where(low, yj, 0.0).astype(BF16)
                ref[0, 2 * i + j] = jnp.concatenate([onehot, kj], axis=1) if ref is ks_ref else kj
    ones_rows = jnp.where(lax.broadcasted_iota(jnp.int32, (V_ROWS - HEAD_DIM, tm), 0) == 0, 1.0, 0.0)
    for part, ref, ck in ((3, vst_ref, SLC_CHUNK), (5, vwt_ref, WIN_CHUNK)):
        for i in range(KVW // LANES):
            vt = tile(part, i).T
            for j in range(2):
                blk = jnp.concatenate([vt[j * HEAD_DIM:(j + 1) * HEAD_DIM], ones_rows], axis=0).astype(BF16)
                for c in range(tm // ck):
                    ref[0, 2 * i + j, c] = blk[:, c * ck:(c + 1) * ck]

    sg = jax.nn.sigmoid(_dot(xb, wg_ref[...]))
    gt_ref[0] = sg.T[:N_KV * GATE_PAD]


def _nsa_prep(x, w_in, tm=512):
    b_, s_, _ = x.shape
    wq = w_in[:, :D_MODEL].astype(BF16)
    wkv = w_in[:, D_MODEL:D_MODEL + 6 * KVW].astype(BF16)
    wg = w_in[:, D_MODEL + 6 * KVW:].reshape(D_MODEL, N_KV, HPG * 3)
    wg = jnp.pad(wg, ((0, 0), (0, 0), (0, GATE_PAD - HPG * 3))).reshape(D_MODEL, N_KV * GATE_PAD)
    wg = jnp.pad(wg, ((0, 0), (0, LANES - N_KV * GATE_PAD))).astype(BF16)
    cos, sin = _rope_tables(jnp.arange(s_))
    const = lambda *shape: _resident(shape, lambda b, i: (0,) * len(shape))
    out_shapes = (
        jax.ShapeDtypeStruct((b_, N_HEADS, HEAD_DIM, s_), BF16),
        jax.ShapeDtypeStruct((b_, N_KV * GATE_PAD, s_), F32),
        jax.ShapeDtypeStruct((b_, N_KV, s_ // CMP_STRIDE, CMP_STRIDE * HEAD_DIM), BF16),
        jax.ShapeDtypeStruct((b_, N_KV, s_ // CMP_STRIDE, CMP_STRIDE * HEAD_DIM), BF16),
        jax.ShapeDtypeStruct((b_, N_KV, s_, 2 * LANES), BF16),
        jax.ShapeDtypeStruct((b_, N_KV, s_ // SLC_CHUNK, V_ROWS, SLC_CHUNK), BF16),
        jax.ShapeDtypeStruct((b_, N_KV, s_, LANES), BF16),
        jax.ShapeDtypeStruct((b_, N_KV, s_ // WIN_CHUNK, V_ROWS, WIN_CHUNK), BF16),
    )
    out_specs = (
        pl.BlockSpec((1, N_HEADS, HEAD_DIM, tm), lambda b, i: (b, 0, 0, i)),
        pl.BlockSpec((1, N_KV * GATE_PAD, tm), lambda b, i: (b, 0, i)),
        pl.BlockSpec((1, N_KV, tm // CMP_STRIDE, CMP_STRIDE * HEAD_DIM), lambda b, i: (b, 0, i, 0)),
        pl.BlockSpec((1, N_KV, tm // CMP_STRIDE, CMP_STRIDE * HEAD_DIM), lambda b, i: (b, 0, i, 0)),
        pl.BlockSpec((1, N_KV, tm, 2 * LANES), lambda b, i: (b, 0, i, 0)),
        pl.BlockSpec((1, N_KV, tm // SLC_CHUNK, V_ROWS, SLC_CHUNK), lambda b, i: (b, 0, i, 0, 0)),
        pl.BlockSpec((1, N_KV, tm, LANES), lambda b, i: (b, 0, i, 0)),
        pl.BlockSpec((1, N_KV, tm // WIN_CHUNK, V_ROWS, WIN_CHUNK), lambda b, i: (b, 0, i, 0, 0)),
    )
    return pl.pallas_call(
        functools.partial(_nsa_prep_body, tm=tm),
        grid=(b_, s_ // tm),
        in_specs=[
            pl.BlockSpec((1, tm, D_MODEL), lambda b, i: (b, i, 0)),
            const(D_MODEL, D_MODEL), const(D_MODEL, 6 * KVW), const(D_MODEL, LANES),
            pl.BlockSpec((tm, LANES), lambda b, i: (i, 0)),
            pl.BlockSpec((tm, LANES), lambda b, i: (i, 0)),
        ],
        out_specs=out_specs,
        out_shape=out_shapes,
        scratch_shapes=[pltpu.VMEM((tm, LANES), F32)],
        compiler_params=pltpu.CompilerParams(
            dimension_semantics=("arbitrary", "arbitrary"), vmem_limit_bytes=VMEM_LIMIT),
        name="nsa_prep",
    )(x, wq, wkv, wg, cos, sin)


def _compress_body(ck_ref, cv_ref, pek_ref, pev_ref, w1k_ref, w1v_ref, w2k_ref, w2v_ref, cos_ref, sin_ref,
                   kcmp_ref, vct_ref, *, n_rows):
    half = CMP_STRIDE * HEAD_DIM

    def mlp(c_ref, pe_ref, w1_ref, w2_ref):
        c = c_ref[0, 0]
        a = _dot(c, w1_ref[:half])
        bnext = pltpu.roll(_dot(c, w1_ref[half:]), n_rows - 1, 0)
        bias = _dot(pe_ref[...], w1_ref[...])[0:1]
        h = _gelu(a + bnext + bias)
        return _dot(h.astype(BF16), w2_ref[...])

    kc = _rope2(mlp(ck_ref, pek_ref, w1k_ref, w2k_ref), cos_ref[...], sin_ref[...]).astype(BF16)
    grp = lax.broadcasted_iota(jnp.int32, (n_rows, 1), 0) // CMP_ROWGRP
    onehot = jnp.where(grp == lax.broadcasted_iota(jnp.int32, (1, LANES), 1), 1.0, 0.0).astype(BF16)
    kcmp_ref[0, 0] = jnp.concatenate([onehot, kc], axis=1)
    vt = mlp(cv_ref, pev_ref, w1v_ref, w2v_ref).T
    ones_rows = jnp.where(lax.broadcasted_iota(jnp.int32, (V_ROWS - HEAD_DIM, n_rows), 0) == 0, 1.0, 0.0)
    vct_ref[0, 0] = jnp.concatenate([vt[:HEAD_DIM], ones_rows], axis=0).astype(BF16)


def _compress(ck, cv, pe_k, w1_k, w2_k, pe_v, w1_v, w2_v):
    b_, g_, n_rows, feat = ck.shape
    cos, sin = _rope_tables(CMP_STRIDE * jnp.arange(n_rows) + CMP_BLOCK - 1)
    pe = lambda p: jnp.pad(p.reshape(1, 2 * feat), ((0, 7), (0, 0))).astype(BF16)
    w1 = lambda w: w.reshape(2 * feat, CMP_HIDDEN).astype(BF16)
    w2 = lambda w: jnp.pad(w, ((0, 0), (0, LANES - HEAD_DIM))).astype(BF16)
    const = lambda *shape: _resident(shape, lambda b, g: (0,) * len(shape))
    blk = lambda *shape: pl.BlockSpec((1, 1) + shape, lambda b, g: (b, g, 0, 0))
    return pl.pallas_call(
        functools.partial(_compress_body, n_rows=n_rows),
        grid=(b_, g_),
        in_specs=[
            blk(n_rows, feat), blk(n_rows, feat),
            const(8, 2 * feat), const(8, 2 * feat),
            const(2 * feat, CMP_HIDDEN), const(2 * feat, CMP_HIDDEN),
            const(CMP_HIDDEN, LANES), const(CMP_HIDDEN, LANES),
            const(n_rows, LANES), const(n_rows, LANES),
        ],
        out_specs=(blk(n_rows, 2 * LANES), blk(V_ROWS, n_rows)),
        out_shape=(jax.ShapeDtypeStruct((b_, g_, n_rows, 2 * LANES), BF16),
                   jax.ShapeDtypeStruct((b_, g_, V_ROWS, n_rows), BF16)),
        compiler_params=pltpu.CompilerParams(
            dimension_semantics=("arbitrary", "arbitrary"), vmem_limit_bytes=VMEM_LIMIT),
        name="nsa_compress",
    )(ck, cv, pe(pe_k), pe(pe_v), w1(w1_k), w1(w1_v), w2(w2_k), w2(w2_v), cos, sin)


def _top_blocks(imp, cur, rows):
    j_i = lax.broadcasted_iota(jnp.int32, (rows, 1), 0)
    forced = (j_i == 0) | (j_i == cur) | (j_i == cur - 1)
    excluded, taken = -3e38, -2e38
    val = jnp.where(forced | (j_i > cur), excluded, imp)
    for _ in range(SLC_TOP - 3):
        vmax = jnp.max(val, axis=0, keepdims=True)
        jmin = jnp.min(jnp.where(val == vmax, j_i, rows), axis=0, keepdims=True)
        val = jnp.where(j_i == jmin, taken, val)
    return jnp.where((forced | (val == taken)) & (j_i <= cur), 0.0, NEG)


def _nsa_attn_body(qt_ref, gt_ref, kcmp_ref, vct_ref, ksa_ref, vst_ref, kw_ref, vwt_ref,
                   o_ref, qa_ref, sc_ref, ps_ref, s_ref, *, n_cmp_rows):
    t_idx = pl.program_id(2)
    t0 = t_idx * TQ
    width = HPG * TQ
    q4 = jnp.concatenate([qt_ref[0, h] for h in range(HPG)], axis=1)
    zeros = jnp.zeros((HEAD_DIM, width), BF16)
    q4p = jnp.concatenate([q4, zeros], axis=0)
    tq = t0 + lax.broadcasted_iota(jnp.int32, (1, width), 1) % TQ
    tq1 = t0 + lax.broadcasted_iota(jnp.int32, (1, TQ), 1)

    def window_branch():
        row_w = lax.broadcasted_iota(jnp.int32, (WIN_CHUNK, 1), 0)
        n_edge_w = TQ // WIN_CHUNK
        n_win = WINDOW // WIN_CHUNK + n_edge_w
        c_first = (t_idx + 1) * n_edge_w - n_win
        c_loads = [jnp.maximum(c_first + w, 0) for w in range(n_win)]
        keys = jnp.concatenate(
            [kw_ref[0, 0, pl.ds(pl.multiple_of(c * WIN_CHUNK, WIN_CHUNK), WIN_CHUNK), :] for c in c_loads], axis=0)
        s_all = _dot(keys, q4p)
        sw, vw = [], []
        for w in range(n_win):
            c_true, c_load = c_first + w, c_loads[w]
            s = s_all[w * WIN_CHUNK:(w + 1) * WIN_CHUNK]
            key = c_true * WIN_CHUNK + row_w
            if w < n_edge_w:
                s = jnp.where(key > jnp.maximum(tq - WINDOW, -1), s, NEG)
            elif w >= n_win - n_edge_w:
                s = jnp.where(key <= tq, s, NEG)
            else:
                s = s + jnp.where(c_true >= 0, 0.0, NEG)
            sw.append(s)
            vw.append(vwt_ref[0, 0, c_load])
        m = sw[0].max(axis=0, keepdims=True)
        for w in range(1, n_win):
            m = jnp.maximum(m, sw[w].max(axis=0, keepdims=True))
        p = jnp.concatenate([jnp.exp2(s - m).astype(BF16) for s in sw], axis=0)
        acc = _dot(jnp.concatenate(vw, axis=1), p)
        return acc[:HEAD_DIM] * (1.0 / acc[HEAD_DIM:HEAD_DIM + 1])

    per = SLC_BLOCK // CMP_STRIDE
    cur = tq1 // SLC_BLOCK
    grp = lax.broadcasted_iota(jnp.int32, (LANES, TQ), 0)
    grp_bias = jnp.where(grp <= t_idx, 0.0, NEG).astype(BF16)
    qc = jnp.concatenate([jnp.concatenate([grp_bias] * HPG, axis=1), q4, zeros], axis=0)
    edge0 = jnp.maximum(t_idx - 1, 0) * CMP_ROWGRP
    edge = pl.ds(pl.multiple_of(edge0, CMP_ROWGRP), 2 * CMP_ROWGRP)
    n_edge = edge0 + lax.broadcasted_iota(jnp.int32, (2 * CMP_ROWGRP, 1), 0)

    def cmp_select(rows):
        def run():
            sc_ref[0:rows, :] = _dot(kcmp_ref[0, 0, 0:rows, :], qc)
            sc_ref[edge, :] = jnp.where(n_edge * CMP_STRIDE + (CMP_BLOCK - 1) <= tq, sc_ref[edge, :], NEG)
            s = sc_ref[0:rows, :]
            m = jnp.max(s, axis=0, keepdims=True)
            p = jnp.exp2(s - m)
            acc = _dot(vct_ref[0, 0, :, 0:rows], p.astype(BF16))
            inv = jnp.where(m > 0.5 * NEG, 1.0 / acc[HEAD_DIM:HEAD_DIM + 1], 0.0)
            pn = p * inv
            ps = pn[:, 0:TQ]
            for h in range(1, HPG):
                ps = ps + pn[:, h * TQ:(h + 1) * TQ]
            for i in range(TQ // LANES):
                ps_ref[i, 0:rows, :] = ps[:, i * LANES:(i + 1) * LANES]
            nb = rows // per
            r = [jnp.concatenate([ps_ref[i, pl.ds(k, nb, stride=per), :] for i in range(TQ // LANES)], axis=1)
                 for k in range(per)]
            first_row = lax.broadcasted_iota(jnp.int32, (nb, 1), 0) == 0
            imp = jnp.where(first_row, 0.0, pltpu.roll(r[per - 1], 1, 0))
            for k in range(per):
                imp = imp + r[k]
            bias = _top_blocks(imp, cur, nb)
            if nb < N_SLC_PAD:
                bias = jnp.concatenate([bias, jnp.full((N_SLC_PAD - nb, TQ), NEG, F32)], axis=0)
            bias = bias.astype(BF16)
            for half in range(2):
                bh = bias[half * SLC_HALF:(half + 1) * SLC_HALF]
                qa_ref[half] = jnp.concatenate([jnp.concatenate([bh] * HPG, axis=1), q4, zeros], axis=0)
            return acc[:HEAD_DIM] * inv
        return run

    o_win = window_branch()
    tiles_per_variant = n_cmp_rows // (CMP_VARIANTS * CMP_ROWGRP)
    o_cmp = lax.switch(t_idx // tiles_per_variant,
                       [cmp_select((k + 1) * n_cmp_rows // CMP_VARIANTS) for k in range(CMP_VARIANTS)])

    chunks_per_half = SLC_HALF * SLC_BLOCK // SLC_CHUNK

    def slc_scores(c, slot):
        keys = ksa_ref[0, 0, pl.ds(pl.multiple_of(c * SLC_CHUNK, SLC_CHUNK), SLC_CHUNK), :]
        sc = _dot(keys, qa_ref[c // chunks_per_half])
        s_ref[slot] = sc
        return jnp.max(sc, axis=0, keepdims=True)

    def slc_update(c, slot, m_i, acc, smax=None):
        sc = s_ref[slot]
        if smax is None:
            key = c * SLC_CHUNK + lax.broadcasted_iota(jnp.int32, (SLC_CHUNK, 1), 0)
            sc = jnp.where(key <= tq, sc, NEG)
            smax = jnp.max(sc, axis=0, keepdims=True)
        m_new = jnp.maximum(m_i, smax)
        p = jnp.exp2(sc - m_new).astype(BF16)
        return m_new, jnp.exp2(m_i - m_new) * acc + _dot(vst_ref[0, 0, c], p)

    def slc_pair(i, carry):
        m_i, acc, smax_a = carry
        smax_b = slc_scores(2 * i + 1, 1)
        m_i, acc = slc_update(2 * i, 0, m_i, acc, smax_a)
        smax_a = slc_scores(2 * i + 2, 0)
        m_i, acc = slc_update(2 * i + 1, 1, m_i, acc, smax_b)
        return m_i, acc, smax_a

    i_diag = t0 // (2 * SLC_CHUNK)
    d0 = 2 * i_diag
    smax = slc_scores(0, 0)
    carry = lax.fori_loop(
        0, i_diag // 2, lambda i, carry: slc_pair(2 * i + 1, slc_pair(2 * i, carry)),
        (jnp.full((1, width), NEG, F32), jnp.zeros((V_ROWS, width), F32), smax))
    m_i, acc, smax = lax.fori_loop(i_diag // 2 * 2, i_diag, slc_pair, carry)

    def tile_in_first():
        return slc_update(d0, 0, m_i, acc)[1]

    def tile_in_second():
        slc_scores(d0 + 1, 1)
        m_mid, acc_mid = slc_update(d0, 0, m_i, acc, smax)
        return slc_update(d0 + 1, 1, m_mid, acc_mid)[1]

    acc = lax.cond(t0 >= (d0 + 1) * SLC_CHUNK, tile_in_second, tile_in_first)
    o_slc = acc[:HEAD_DIM] * (1.0 / acc[HEAD_DIM:HEAD_DIM + 1])

    gates = gt_ref[0]
    outs = []
    for h in range(HPG):
        cols = slice(h * TQ, (h + 1) * TQ)
        outs.append(gates[3 * h:3 * h + 1] * o_cmp[:, cols]
                    + gates[3 * h + 1:3 * h + 2] * o_slc[:, cols]
                    + gates[3 * h + 2:3 * h + 3] * o_win[:, cols])
    o_ref[0] = jnp.concatenate(outs, axis=0).T


def _nsa_attn(qt, gt, kcmp, vct, ks, vst, kw, vwt):
    b_, _, _, s_ = qt.shape
    n_cmp_rows = kcmp.shape[2]
    assert s_ % (2 * SLC_CHUNK) == 0 and (SLC_HALF * SLC_BLOCK) % SLC_CHUNK == 0 and s_ // SLC_BLOCK <= N_SLC_PAD and n_cmp_rows <= CMP_ROWGRP * LANES
    assert TQ % WIN_CHUNK == 0 and WINDOW % WIN_CHUNK == 0 and WINDOW >= TQ and SLC_CHUNK % TQ == 0
    per_group = lambda *shape: _resident(
        (1, 1) + shape, lambda b, g, t: (b, g) + (0,) * len(shape))
    return pl.pallas_call(
        functools.partial(_nsa_attn_body, n_cmp_rows=n_cmp_rows),
        grid=(b_, N_KV, s_ // TQ),
        in_specs=[
            pl.BlockSpec((1, HPG, HEAD_DIM, TQ), lambda b, g, t: (b, g, 0, t)),
            pl.BlockSpec((1, GATE_PAD, TQ), lambda b, g, t: (b, g, t)),
            per_group(n_cmp_rows, 2 * LANES),
            per_group(V_ROWS, n_cmp_rows),
            per_group(s_, 2 * LANES),
            per_group(s_ // SLC_CHUNK, V_ROWS, SLC_CHUNK),
            per_group(s_, LANES),
            per_group(s_ // WIN_CHUNK, V_ROWS, WIN_CHUNK),
        ],
        out_specs=pl.BlockSpec((1, TQ, HPG * HEAD_DIM), lambda b, g, t: (b, t, g)),
        out_shape=jax.ShapeDtypeStruct((b_, s_, N_HEADS * HEAD_DIM), F32),
        scratch_shapes=[pltpu.VMEM((2, 2 * LANES, HPG * TQ), BF16),
                        pltpu.VMEM((n_cmp_rows, HPG * TQ), F32),
                        pltpu.VMEM((TQ // LANES, n_cmp_rows, LANES), F32),
                        pltpu.VMEM((2, SLC_CHUNK, HPG * TQ), F32)],
        compiler_params=pltpu.CompilerParams(
            dimension_semantics=("arbitrary", "arbitrary", "arbitrary"), vmem_limit_bytes=VMEM_LIMIT),
        name="nsa_attn",
    )(qt, gt, kcmp, vct, ks, vst, kw, vwt)


def _nsa_heads(x, w_in, pe_k, w1_k, w2_k, pe_v, w1_v, w2_v):
    qt, gt, kc, vc, ks, vst, kw, vwt = _nsa_prep(x, w_in)
    kcmp, vct = _compress(kc, vc, pe_k, w1_k, w2_k, pe_v, w1_v, w2_v)
    return _nsa_attn(qt, gt, kcmp, vct, ks, vst, kw, vwt)


def kernel(x, l0_ffn1_w_in, l0_ffn1_w_out, l0_ln1_g, l0_ln1_b, l0_gm_w_in, l0_gm_ln_g, l0_gm_ln_b, l0_gm_w_s, l0_gm_b_s, l0_gm_w_out, l0_ln2_g, l0_ln2_b, l0_ffn2_w_in, l0_ffn2_w_out, l0_ln3_g, l0_ln3_b, l1_ffn1_w_in, l1_ffn1_w_out, l1_ln1_g, l1_ln1_b, l1_nsa_w_in, l1_nsa_cmp_pe_k, l1_nsa_cmp_w1_k, l1_nsa_cmp_w2_k, l1_nsa_cmp_pe_v, l1_nsa_cmp_w1_v, l1_nsa_cmp_w2_v, l1_nsa_w_out, l1_ln2_g, l1_ln2_b, l1_ffn2_w_in, l1_ffn2_w_out, l1_ln3_g, l1_ln3_b):
    b_, s_, d_ = x.shape
    h = x.reshape(b_ * s_, d_)
    h = _ffn_ln(h, l0_ffn1_w_in, l0_ffn1_w_out, l0_ln1_g, l0_ln1_b)
    h = _gmlp_ln(h, l0_gm_w_in, l0_gm_ln_g, l0_gm_ln_b, l0_gm_w_s, l0_gm_b_s, l0_gm_w_out, l0_ln2_g, l0_ln2_b)
    h = _ffn_ln(h, l0_ffn2_w_in, l0_ffn2_w_out, l0_ln3_g, l0_ln3_b)
    h = _ffn_ln(h, l1_ffn1_w_in, l1_ffn1_w_out, l1_ln1_g, l1_ln1_b)
    heads = _nsa_heads(h.reshape(b_, s_, d_), l1_nsa_w_in, l1_nsa_cmp_pe_k, l1_nsa_cmp_w1_k, l1_nsa_cmp_w2_k,
                       l1_nsa_cmp_pe_v, l1_nsa_cmp_w1_v, l1_nsa_cmp_w2_v)
    h = _ffn_ln(h, l1_ffn2_w_in, l1_ffn2_w_out, l1_ln3_g, l1_ln3_b,
                mix=(heads.reshape(b_ * s_, -1), l1_nsa_w_out, l1_ln2_g, l1_ln2_b))
    return h.reshape(b_, s_, d_)
```

```python
import functools

import jax
import jax.numpy as jnp
from jax import lax
from jax.experimental import pallas as pl
from jax.experimental.pallas import tpu as pltpu

F32 = jnp.float32
BF16 = jnp.bfloat16

D_MODEL = 1024
DEPTH = 2
ALPHA = (2 * DEPTH) ** 0.25
LN_EPS = 1e-5
D_FF = 2816
GM_WIDTH = 3 * D_MODEL
GM_CHUNK = 128
GM_GROUPS = 16
GM_GCH = GM_WIDTH // GM_GROUPS
HEAD_DIM = 64
N_HEADS = D_MODEL // HEAD_DIM
N_KV = 4
HPG = N_HEADS // N_KV
CMP_BLOCK = 32
CMP_STRIDE = 16
CMP_HIDDEN = 256
SLC_BLOCK = 64
SLC_TOP = 16
WINDOW = 512
ROPE_THETA = 10000.0
NEG = -1e30
KVW = N_KV * HEAD_DIM

LANES = 128
TQ = 256
SLC_CHUNK = 512
WIN_CHUNK = 128
CMP_ROWGRP = TQ // CMP_STRIDE
CMP_VARIANTS = 8
SLC_HALF = 128
N_SLC_PAD = 2 * SLC_HALF
V_ROWS = 80
LOG2E = 1.4426950408889634
VMEM_LIMIT = 56 * 1024 * 1024


def _resident(shape, index_map):
    return pl.BlockSpec(shape, index_map, pipeline_mode=pl.Buffered(1))


def _layer_norm(y, g, b):
    mu = jnp.mean(y, axis=-1, keepdims=True)
    d = y - mu
    var = jnp.mean(d * d, axis=-1, keepdims=True)
    return d * lax.rsqrt(var + LN_EPS) * g + b


def _gelu(x):
    return 0.5 * x * (1.0 + jnp.tanh(0.7978845608028654 * (x + 0.044715 * (x * x * x))))


def _dot(a, b):
    return jnp.dot(a, b, preferred_element_type=F32)


FFN_COLS = 256


def _ffn_body(*refs, mixed):
    if mixed:
        x_ref, mix_ref, wmix_ref, gmix_ref, bmix_ref, win_ref, wout_ref, g_ref, b_ref, o_ref, a_ref = refs
        x = _layer_norm(ALPHA * x_ref[...] + _dot(mix_ref[...].astype(BF16), wmix_ref[...]),
                        gmix_ref[...], bmix_ref[...])
    else:
        x_ref, win_ref, wout_ref, g_ref, b_ref, o_ref, a_ref = refs
        x = x_ref[...]
    xb = x.astype(BF16)
    for c in range(D_FF // FFN_COLS):
        lo = c * FFN_COLS
        gate = _dot(xb, win_ref[:, lo:lo + FFN_COLS])
        up = _dot(xb, win_ref[:, D_FF + lo:D_FF + lo + FFN_COLS])
        a_ref[:, lo:lo + FFN_COLS] = (gate * jax.nn.sigmoid(gate) * up).astype(BF16)
    y = _dot(a_ref[...], wout_ref[...])
    o_ref[...] = _layer_norm(ALPHA * x + 0.5 * y, g_ref[...], b_ref[...])


def _ffn_ln(x2, w_in, w_out, g, b, mix=None, tm=512):
    m = x2.shape[0]
    rows = pl.BlockSpec((tm, D_MODEL), lambda i: (i, 0))
    vec = _resident((1, D_MODEL), lambda i: (0, 0))
    mix_specs, mix_args = [], []
    if mix is not None:
        mix_out, w_mix, g_mix, b_mix = mix
        mix_specs = [rows, _resident((D_MODEL, D_MODEL), lambda i: (0, 0)), vec, vec]
        mix_args = [mix_out, w_mix.astype(BF16), g_mix.reshape(1, -1), b_mix.reshape(1, -1)]
    return pl.pallas_call(
        functools.partial(_ffn_body, mixed=mix is not None),
        grid=(m // tm,),
        in_specs=[rows] + mix_specs + [
            _resident((D_MODEL, 2 * D_FF), lambda i: (0, 0)),
            _resident((D_FF, D_MODEL), lambda i: (0, 0)),
            vec, vec,
        ],
        out_specs=rows,
        out_shape=jax.ShapeDtypeStruct((m, D_MODEL), F32),
        scratch_shapes=[pltpu.VMEM((tm, D_FF), BF16)],
        compiler_params=pltpu.CompilerParams(
            dimension_semantics=("arbitrary",), vmem_limit_bytes=VMEM_LIMIT),
        name="mix_ffn_ln" if mix is not None else "ffn_ln",
    )(x2, *mix_args, w_in.astype(BF16), w_out.astype(BF16), g.reshape(1, -1), b.reshape(1, -1))


GM_PAIR = 2 * GM_GCH
GM_COLS = 256


def _gmlp_body(x_ref, win_ref, lng_ref, lnb_ref, ws_ref, bs_ref, wout_ref, g_ref, b_ref, o_ref,
               v_ref, a_ref, *, tm):
    x = x_ref[...]
    xb = x.astype(BF16)
    for c in range(GM_WIDTH // GM_COLS):
        lo = GM_WIDTH + c * GM_COLS
        v_ref[:, c * GM_COLS:(c + 1) * GM_COLS] = _gelu(_dot(xb, win_ref[:, lo:lo + GM_COLS]))
    v_ref[...] = _layer_norm(v_ref[...], lng_ref[...], lnb_ref[...])

    row = lax.broadcasted_iota(jnp.int32, (GM_CHUNK, GM_CHUNK), 0)
    col = lax.broadcasted_iota(jnp.int32, (GM_CHUNK, GM_CHUNK), 1)
    causal = row >= col
    first = lax.broadcasted_iota(jnp.int32, (1, GM_PAIR), 1) < GM_GCH
    for p in range(GM_GROUPS // 2):
        lo = p * GM_PAIR
        w0 = jnp.where(causal, ws_ref[2 * p], 0.0).astype(BF16)
        w1 = jnp.where(causal, ws_ref[2 * p + 1], 0.0).astype(BF16)
        wcat = jnp.concatenate([w0, w1], axis=1)
        if p % 2 == 0:
            u2 = _gelu(_dot(xb, win_ref[:, lo:lo + 2 * GM_PAIR]))
        u = u2[:, (p % 2) * GM_PAIR:(p % 2 + 1) * GM_PAIR]
        bias = bs_ref[:, lo:lo + GM_PAIR]
        stacked = []
        for n in range(tm // GM_CHUNK):
            blk = v_ref[n * GM_CHUNK:(n + 1) * GM_CHUNK, lo:lo + GM_PAIR]
            stacked.append(jnp.concatenate([jnp.where(first, blk, 0.0).astype(BF16),
                                            jnp.where(first, 0.0, blk).astype(BF16)], axis=0))
        sp = _dot(wcat, jnp.concatenate(stacked, axis=1))
        for n in range(tm // GM_CHUNK):
            r0, r1 = n * GM_CHUNK, (n + 1) * GM_CHUNK
            a_ref[r0:r1, lo:lo + GM_PAIR] = (u[r0:r1] * (sp[:, n * GM_PAIR:(n + 1) * GM_PAIR] + bias)).astype(BF16)
    y = _dot(a_ref[...], wout_ref[...])
    o_ref[...] = _layer_norm(ALPHA * x + y, g_ref[...], b_ref[...])


def _gmlp_ln(x2, w_in, ln_g, ln_b, w_s, b_s, w_out, g, b, tm=512):
    m = x2.shape[0]
    bias = jnp.repeat(b_s.T, GM_GCH, axis=1)
    const = lambda *shape: _resident(shape, lambda i: (0,) * len(shape))
    return pl.pallas_call(
        functools.partial(_gmlp_body, tm=tm),
        grid=(m // tm,),
        in_specs=[
            pl.BlockSpec((tm, D_MODEL), lambda i: (i, 0)),
            const(D_MODEL, 2 * GM_WIDTH),
            const(1, GM_WIDTH), const(1, GM_WIDTH),
            const(GM_GROUPS, GM_CHUNK, GM_CHUNK),
            const(GM_CHUNK, GM_WIDTH),
            const(GM_WIDTH, D_MODEL),
            const(1, D_MODEL), const(1, D_MODEL),
        ],
        out_specs=pl.BlockSpec((tm, D_MODEL), lambda i: (i, 0)),
        out_shape=jax.ShapeDtypeStruct((m, D_MODEL), F32),
        scratch_shapes=[pltpu.VMEM((tm, GM_WIDTH), F32), pltpu.VMEM((tm, GM_WIDTH), BF16)],
        compiler_params=pltpu.CompilerParams(
            dimension_semantics=("arbitrary",), vmem_limit_bytes=VMEM_LIMIT),
        name="gmlp_ln",
    )(x2, w_in.astype(BF16), ln_g.reshape(1, -1), ln_b.reshape(1, -1), w_s, bias,
      w_out.astype(BF16), g.reshape(1, -1), b.reshape(1, -1))


GATE_PAD = 16


def _rope_tables(pos):
    half = HEAD_DIM // 2
    freq = ROPE_THETA ** (-jnp.arange(half, dtype=F32) / half)
    ang = pos.astype(F32)[:, None] * freq
    cos, sin = jnp.cos(ang), jnp.sin(ang)
    return jnp.tile(cos, (1, 4)), jnp.tile(jnp.concatenate([-sin, sin], axis=1), (1, 2))


def _rope2(y, cos, sin):
    lane = lax.broadcasted_iota(jnp.int32, (1, LANES), 1)
    first = (lane % HEAD_DIM) < (HEAD_DIM // 2)
    partner = jnp.where(first, pltpu.roll(y, LANES - HEAD_DIM // 2, 1), pltpu.roll(y, HEAD_DIM // 2, 1))
    return y * cos + partner * sin


def _nsa_prep_body(x_ref, wq_ref, wkv_ref, wg_ref, cos_ref, sin_ref,
                   qt_ref, gt_ref, kc_ref, vc_ref, ks_ref, vst_ref, kw_ref, vwt_ref, st_ref, *, tm):
    xb = x_ref[0].astype(BF16)
    cos, sin = cos_ref[...], sin_ref[...]
    lane = lax.broadcasted_iota(jnp.int32, (1, LANES), 1)
    low = lane < HEAD_DIM
    scale = HEAD_DIM ** -0.5 * LOG2E

    yq = _dot(xb, wq_ref[...])
    for i in range(D_MODEL // LANES):
        qtt = (_rope2(yq[:, i * LANES:(i + 1) * LANES], cos, sin) * scale).T
        qt_ref[0, 2 * i] = qtt[:HEAD_DIM].astype(BF16)
        qt_ref[0, 2 * i + 1] = qtt[HEAD_DIM:].astype(BF16)

    ykv = _dot(xb, wkv_ref[...])
    tile = lambda part, i: ykv[:, part * KVW + i * LANES:part * KVW + (i + 1) * LANES]
    for part, ref in ((0, kc_ref), (1, vc_ref)):
        for i in range(KVW // LANES):
            st_ref[...] = tile(part, i)
            tok = [st_ref[pl.ds(l, tm // CMP_STRIDE, stride=CMP_STRIDE), :] for l in range(CMP_STRIDE)]
            for j in range(2):
                cols = []
                for c in range(CMP_STRIDE // 2):
                    a, b = tok[2 * c], tok[2 * c + 1]
                    cols.append(jnp.where(low, a, pltpu.roll(b, HEAD_DIM, 1)) if j == 0
                                else jnp.where(low, pltpu.roll(a, HEAD_DIM, 1), b))
                ref[0, 2 * i + j] = jnp.concatenate(cols, axis=1).astype(BF16)
    pos = pl.program_id(1) * tm + lax.broadcasted_iota(jnp.int32, (tm, 1), 0)
    onehot = jnp.where((pos // SLC_BLOCK) % SLC_HALF == lane, 1.0, 0.0).astype(BF16)
    for part, ref in ((2, ks_ref), (4, kw_ref)):
        for i in range(KVW // LANES):
            y = _rope2(tile(part, i), cos, sin)
            for j, yj in enumerate((y, pltpu.roll(y, HEAD_DIM, 1))):
                kj = jnp.where(low, yj, 0.0).astype(BF16)
                ref[0, 2 * i + j] = jnp.concatenate([onehot, kj], axis=1) if ref is ks_ref else kj
    ones_rows = jnp.where(lax.broadcasted_iota(jnp.int32, (V_ROWS - HEAD_DIM, tm), 0) == 0, 1.0, 0.0)
    for part, ref, ck in ((3, vst_ref, SLC_CHUNK), (5, vwt_ref, WIN_CHUNK)):
        for i in range(KVW // LANES):
            vt = tile(part, i).T
            for j in range(2):
                blk = jnp.concatenate([vt[j * HEAD_DIM:(j + 1) * HEAD_DIM], ones_rows], axis=0).astype(BF16)
                for c in range(tm // ck):
                    ref[0, 2 * i + j, c] = blk[:, c * ck:(c + 1) * ck]

    sg = jax.nn.sigmoid(_dot(xb, wg_ref[...]))
    gt_ref[0] = sg.T[:N_KV * GATE_PAD]


def _nsa_prep(x, w_in, tm=512):
    b_, s_, _ = x.shape
    wq = w_in[:, :D_MODEL].astype(BF16)
    wkv = w_in[:, D_MODEL:D_MODEL + 6 * KVW].astype(BF16)
    wg = w_in[:, D_MODEL + 6 * KVW:].reshape(D_MODEL, N_KV, HPG * 3)
    wg = jnp.pad(wg, ((0, 0), (0, 0), (0, GATE_PAD - HPG * 3))).reshape(D_MODEL, N_KV * GATE_PAD)
    wg = jnp.pad(wg, ((0, 0), (0, LANES - N_KV * GATE_PAD))).astype(BF16)
    cos, sin = _rope_tables(jnp.arange(s_))
    const = lambda *shape: _resident(shape, lambda b, i: (0,) * len(shape))
    out_shapes = (
        jax.ShapeDtypeStruct((b_, N_HEADS, HEAD_DIM, s_), BF16),
        jax.ShapeDtypeStruct((b_, N_KV * GATE_PAD, s_), F32),
        jax.ShapeDtypeStruct((b_, N_KV, s_ // CMP_STRIDE, CMP_STRIDE * HEAD_DIM), BF16),
        jax.ShapeDtypeStruct((b_, N_KV, s_ // CMP_STRIDE, CMP_STRIDE * HEAD_DIM), BF16),
        jax.ShapeDtypeStruct((b_, N_KV, s_, 2 * LANES), BF16),
        jax.ShapeDtypeStruct((b_, N_KV, s_ // SLC_CHUNK, V_ROWS, SLC_CHUNK), BF16),
        jax.ShapeDtypeStruct((b_, N_KV, s_, LANES), BF16),
        jax.ShapeDtypeStruct((b_, N_KV, s_ // WIN_CHUNK, V_ROWS, WIN_CHUNK), BF16),
    )
    out_specs = (
        pl.BlockSpec((1, N_HEADS, HEAD_DIM, tm), lambda b, i: (b, 0, 0, i)),
        pl.BlockSpec((1, N_KV * GATE_PAD, tm), lambda b, i: (b, 0, i)),
        pl.BlockSpec((1, N_KV, tm // CMP_STRIDE, CMP_STRIDE * HEAD_DIM), lambda b, i: (b, 0, i, 0)),
        pl.BlockSpec((1, N_KV, tm // CMP_STRIDE, CMP_STRIDE * HEAD_DIM), lambda b, i: (b, 0, i, 0)),
        pl.BlockSpec((1, N_KV, tm, 2 * LANES), lambda b, i: (b, 0, i, 0)),
        pl.BlockSpec((1, N_KV, tm // SLC_CHUNK, V_ROWS, SLC_CHUNK), lambda b, i: (b, 0, i, 0, 0)),
        pl.BlockSpec((1, N_KV, tm, LANES), lambda b, i: (b, 0, i, 0)),
        pl.BlockSpec((1, N_KV, tm // WIN_CHUNK, V_ROWS, WIN_CHUNK), lambda b, i: (b, 0, i, 0, 0)),
    )
    return pl.pallas_call(
        functools.partial(_nsa_prep_body, tm=tm),
        grid=(b_, s_ // tm),
        in_specs=[
            pl.BlockSpec((1, tm, D_MODEL), lambda b, i: (b, i, 0)),
            const(D_MODEL, D_MODEL), const(D_MODEL, 6 * KVW), const(D_MODEL, LANES),
            pl.BlockSpec((tm, LANES), lambda b, i: (i, 0)),
            pl.BlockSpec((tm, LANES), lambda b, i: (i, 0)),
        ],
        out_specs=out_specs,
        out_shape=out_shapes,
        scratch_shapes=[pltpu.VMEM((tm, LANES), F32)],
        compiler_params=pltpu.CompilerParams(
            dimension_semantics=("arbitrary", "arbitrary"), vmem_limit_bytes=VMEM_LIMIT),
        name="nsa_prep",
    )(x, wq, wkv, wg, cos, sin)


def _compress_body(ck_ref, cv_ref, pek_ref, pev_ref, w1k_ref, w1v_ref, w2k_ref, w2v_ref, cos_ref, sin_ref,
                   kcmp_ref, vct_ref, *, n_rows):
    half = CMP_STRIDE * HEAD_DIM

    def mlp(c_ref, pe_ref, w1_ref, w2_ref):
        c = c_ref[0, 0]
        a = _dot(c, w1_ref[:half])
        bnext = pltpu.roll(_dot(c, w1_ref[half:]), n_rows - 1, 0)
        bias = _dot(pe_ref[...], w1_ref[...])[0:1]
        h = _gelu(a + bnext + bias)
        return _dot(h.astype(BF16), w2_ref[...])

    kc = _rope2(mlp(ck_ref, pek_ref, w1k_ref, w2k_ref), cos_ref[...], sin_ref[...]).astype(BF16)
    grp = lax.broadcasted_iota(jnp.int32, (n_rows, 1), 0) // CMP_ROWGRP
    onehot = jnp.where(grp == lax.broadcasted_iota(jnp.int32, (1, LANES), 1), 1.0, 0.0).astype(BF16)
    kcmp_ref[0, 0] = jnp.concatenate([onehot, kc], axis=1)
    vt = mlp(cv_ref, pev_ref, w1v_ref, w2v_ref).T
    ones_rows = jnp.where(lax.broadcasted_iota(jnp.int32, (V_ROWS - HEAD_DIM, n_rows), 0) == 0, 1.0, 0.0)
    vct_ref[0, 0] = jnp.concatenate([vt[:HEAD_DIM], ones_rows], axis=0).astype(BF16)


def _compress(ck, cv, pe_k, w1_k, w2_k, pe_v, w1_v, w2_v):
    b_, g_, n_rows, feat = ck.shape
    cos, sin = _rope_tables(CMP_STRIDE * jnp.arange(n_rows) + CMP_BLOCK - 1)
    pe = lambda p: jnp.pad(p.reshape(1, 2 * feat), ((0, 7), (0, 0))).astype(BF16)
    w1 = lambda w: w.reshape(2 * feat, CMP_HIDDEN).astype(BF16)
    w2 = lambda w: jnp.pad(w, ((0, 0), (0, LANES - HEAD_DIM))).astype(BF16)
    const = lambda *shape: _resident(shape, lambda b, g: (0,) * len(shape))
    blk = lambda *shape: pl.BlockSpec((1, 1) + shape, lambda b, g: (b, g, 0, 0))
    return pl.pallas_call(
        functools.partial(_compress_body, n_rows=n_rows),
        grid=(b_, g_),
        in_specs=[
            blk(n_rows, feat), blk(n_rows, feat),
            const(8, 2 * feat), const(8, 2 * feat),
            const(2 * feat, CMP_HIDDEN), const(2 * feat, CMP_HIDDEN),
            const(CMP_HIDDEN, LANES), const(CMP_HIDDEN, LANES),
            const(n_rows, LANES), const(n_rows, LANES),
        ],
        out_specs=(blk(n_rows, 2 * LANES), blk(V_ROWS, n_rows)),
        out_shape=(jax.ShapeDtypeStruct((b_, g_, n_rows, 2 * LANES), BF16),
                   jax.ShapeDtypeStruct((b_, g_, V_ROWS, n_rows), BF16)),
        compiler_params=pltpu.CompilerParams(
            dimension_semantics=("arbitrary", "arbitrary"), vmem_limit_bytes=VMEM_LIMIT),
        name="nsa_compress",
    )(ck, cv, pe(pe_k), pe(pe_v), w1(w1_k), w1(w1_v), w2(w2_k), w2(w2_v), cos, sin)


def _top_blocks(imp, cur, rows):
    j_i = lax.broadcasted_iota(jnp.int32, (rows, 1), 0)
    forced = (j_i == 0) | (j_i == cur) | (j_i == cur - 1)
    excluded, taken = -3e38, -2e38
    val = jnp.where(forced | (j_i > cur), excluded, imp)
    for _ in range(SLC_TOP - 3):
        vmax = jnp.max(val, axis=0, keepdims=True)
        jmin = jnp.min(jnp.where(val == vmax, j_i, rows), axis=0, keepdims=True)
        val = jnp.where(j_i == jmin, taken, val)
    return jnp.where((forced | (val == taken)) & (j_i <= cur), 0.0, NEG)


def _nsa_attn_body(qt_ref, gt_ref, kcmp_ref, vct_ref, ksa_ref, vst_ref, kw_ref, vwt_ref,
                   o_ref, qa_ref, sc_ref, ps_ref, s_ref, *, n_cmp_rows):
    t_idx = pl.program_id(2)
    t0 = t_idx * TQ
    width = HPG * TQ
    q4 = jnp.concatenate([qt_ref[0, h] for h in range(HPG)], axis=1)
    zeros = jnp.zeros((HEAD_DIM, width), BF16)
    q4p = jnp.concatenate([q4, zeros], axis=0)
    tq = t0 + lax.broadcasted_iota(jnp.int32, (1, width), 1) % TQ
    tq1 = t0 + lax.broadcasted_iota(jnp.int32, (1, TQ), 1)

    def window_branch():
        row_w = lax.broadcasted_iota(jnp.int32, (WIN_CHUNK, 1), 0)
        n_edge_w = TQ // WIN_CHUNK
        n_win = WINDOW // WIN_CHUNK + n_edge_w
        c_first = (t_idx + 1) * n_edge_w - n_win
        c_loads = [jnp.maximum(c_first + w, 0) for w in range(n_win)]
        keys = jnp.concatenate(
            [kw_ref[0, 0, pl.ds(pl.multiple_of(c * WIN_CHUNK, WIN_CHUNK), WIN_CHUNK), :] for c in c_loads], axis=0)
        s_all = _dot(keys, q4p)
        sw, vw = [], []
        for w in range(n_win):
            c_true, c_load = c_first + w, c_loads[w]
            s = s_all[w * WIN_CHUNK:(w + 1) * WIN_CHUNK]
            key = c_true * WIN_CHUNK + row_w
            if w < n_edge_w:
                s = jnp.where(key > jnp.maximum(tq - WINDOW, -1), s, NEG)
            elif w >= n_win - n_edge_w:
                s = jnp.where(key <= tq, s, NEG)
            else:
                s = s + jnp.where(c_true >= 0, 0.0, NEG)
            sw.append(s)
            vw.append(vwt_ref[0, 0, c_load])
        m = sw[0].max(axis=0, keepdims=True)
        for w in range(1, n_win):
            m = jnp.maximum(m, sw[w].max(axis=0, keepdims=True))
        p = jnp.concatenate([jnp.exp2(s - m).astype(BF16) for s in sw], axis=0)
        acc = _dot(jnp.concatenate(vw, axis=1), p)
        return acc[:HEAD_DIM] * (1.0 / acc[HEAD_DIM:HEAD_DIM + 1])

    per = SLC_BLOCK // CMP_STRIDE
    cur = tq1 // SLC_BLOCK
    grp = lax.broadcasted_iota(jnp.int32, (LANES, TQ), 0)
    grp_bias = jnp.where(grp <= t_idx, 0.0, NEG).astype(BF16)
    qc = jnp.concatenate([jnp.concatenate([grp_bias] * HPG, axis=1), q4, zeros], axis=0)
    edge0 = jnp.maximum(t_idx - 1, 0) * CMP_ROWGRP
    edge = pl.ds(pl.multiple_of(edge0, CMP_ROWGRP), 2 * CMP_ROWGRP)
    n_edge = edge0 + lax.broadcasted_iota(jnp.int32, (2 * CMP_ROWGRP, 1), 0)

    def cmp_select(rows):
        def run():
            sc_ref[0:rows, :] = _dot(kcmp_ref[0, 0, 0:rows, :], qc)
            sc_ref[edge, :] = jnp.where(n_edge * CMP_STRIDE + (CMP_BLOCK - 1) <= tq, sc_ref[edge, :], NEG)
            s = sc_ref[0:rows, :]
            m = jnp.max(s, axis=0, keepdims=True)
            p = jnp.exp2(s - m)
            acc = _dot(vct_ref[0, 0, :, 0:rows], p.astype(BF16))
            inv = jnp.where(m > 0.5 * NEG, 1.0 / acc[HEAD_DIM:HEAD_DIM + 1], 0.0)
            pn = p * inv
            ps = pn[:, 0:TQ]
            for h in range(1, HPG):
                ps = ps + pn[:, h * TQ:(h + 1) * TQ]
            for i in range(TQ // LANES):
                ps_ref[i, 0:rows, :] = ps[:, i * LANES:(i + 1) * LANES]
            nb = rows // per
            r = [jnp.concatenate([ps_ref[i, pl.ds(k, nb, stride=per), :] for i in range(TQ // LANES)], axis=1)
                 for k in range(per)]
            first_row = lax.broadcasted_iota(jnp.int32, (nb, 1), 0) == 0
            imp = jnp.where(first_row, 0.0, pltpu.roll(r[per - 1], 1, 0))
            for k in range(per):
                imp = imp + r[k]
            bias = _top_blocks(imp, cur, nb)
            if nb < N_SLC_PAD:
                bias = jnp.concatenate([bias, jnp.full((N_SLC_PAD - nb, TQ), NEG, F32)], axis=0)
            bias = bias.astype(BF16)
            for half in range(2):
                bh = bias[half * SLC_HALF:(half + 1) * SLC_HALF]
                qa_ref[half] = jnp.concatenate([jnp.concatenate([bh] * HPG, axis=1), q4, zeros], axis=0)
            return acc[:HEAD_DIM] * inv
        return run

    o_win = window_branch()
    tiles_per_variant = n_cmp_rows // (CMP_VARIANTS * CMP_ROWGRP)
    o_cmp = lax.switch(t_idx // tiles_per_variant,
                       [cmp_select((k + 1) * n_cmp_rows // CMP_VARIANTS) for k in range(CMP_VARIANTS)])

    chunks_per_half = SLC_HALF * SLC_BLOCK // SLC_CHUNK

    def slc_scores(c, slot):
        keys = ksa_ref[0, 0, pl.ds(pl.multiple_of(c * SLC_CHUNK, SLC_CHUNK), SLC_CHUNK), :]
        sc = _dot(keys, qa_ref[c // chunks_per_half])
        s_ref[slot] = sc
        return jnp.max(sc, axis=0, keepdims=True)

    def slc_update(c, slot, m_i, acc, smax=None):
        sc = s_ref[slot]
        if smax is None:
            key = c * SLC_CHUNK + lax.broadcasted_iota(jnp.int32, (SLC_CHUNK, 1), 0)
            sc = jnp.where(key <= tq, sc, NEG)
            smax = jnp.max(sc, axis=0, keepdims=True)
        m_new = jnp.maximum(m_i, smax)
        p = jnp.exp2(sc - m_new).astype(BF16)
        return m_new, jnp.exp2(m_i - m_new) * acc + _dot(vst_ref[0, 0, c], p)

    def slc_pair(i, carry):
        m_i, acc, smax_a = carry
        smax_b = slc_scores(2 * i + 1, 1)
        m_i, acc = slc_update(2 * i, 0, m_i, acc, smax_a)
        smax_a = slc_scores(2 * i + 2, 0)
        m_i, acc = slc_update(2 * i + 1, 1, m_i, acc, smax_b)
        return m_i, acc, smax_a

    i_diag = t0 // (2 * SLC_CHUNK)
    d0 = 2 * i_diag
    smax = slc_scores(0, 0)
    def slc_pairs(n):
        def trip(i, carry):
            for k in range(n):
                carry = slc_pair(n * i + k, carry)
            return carry
        return trip

    carry = (jnp.full((1, width), NEG, F32), jnp.zeros((V_ROWS, width), F32), smax)
    carry = lax.fori_loop(0, i_diag // 4, slc_pairs(4), carry)
    carry = lax.fori_loop(i_diag // 4 * 2, i_diag // 2, slc_pairs(2), carry)
    m_i, acc, smax = lax.fori_loop(i_diag // 2 * 2, i_diag, slc_pair, carry)

    def tile_in_first():
        return slc_update(d0, 0, m_i, acc)[1]

    def tile_in_second():
        slc_scores(d0 + 1, 1)
        m_mid, acc_mid = slc_update(d0, 0, m_i, acc, smax)
        return slc_update(d0 + 1, 1, m_mid, acc_mid)[1]

    acc = lax.cond(t0 >= (d0 + 1) * SLC_CHUNK, tile_in_second, tile_in_first)
    o_slc = acc[:HEAD_DIM] * (1.0 / acc[HEAD_DIM:HEAD_DIM + 1])

    gates = gt_ref[0]
    outs = []
    for h in range(HPG):
        cols = slice(h * TQ, (h + 1) * TQ)
        outs.append(gates[3 * h:3 * h + 1] * o_cmp[:, cols]
                    + gates[3 * h + 1:3 * h + 2] * o_slc[:, cols]
                    + gates[3 * h + 2:3 * h + 3] * o_win[:, cols])
    o_ref[0] = jnp.concatenate(outs, axis=0).T


def _nsa_attn(qt, gt, kcmp, vct, ks, vst, kw, vwt):
    b_, _, _, s_ = qt.shape
    n_cmp_rows = kcmp.shape[2]
    assert s_ % (2 * SLC_CHUNK) == 0 and (SLC_HALF * SLC_BLOCK) % SLC_CHUNK == 0 and s_ // SLC_BLOCK <= N_SLC_PAD and n_cmp_rows <= CMP_ROWGRP * LANES
    assert TQ % WIN_CHUNK == 0 and WINDOW % WIN_CHUNK == 0 and WINDOW >= TQ and SLC_CHUNK % TQ == 0
    per_group = lambda *shape: _resident(
        (1, 1) + shape, lambda b, g, t: (b, g) + (0,) * len(shape))
    return pl.pallas_call(
        functools.partial(_nsa_attn_body, n_cmp_rows=n_cmp_rows),
        grid=(b_, N_KV, s_ // TQ),
        in_specs=[
            pl.BlockSpec((1, HPG, HEAD_DIM, TQ), lambda b, g, t: (b, g, 0, t)),
            pl.BlockSpec((1, GATE_PAD, TQ), lambda b, g, t: (b, g, t)),
            per_group(n_cmp_rows, 2 * LANES),
            per_group(V_ROWS, n_cmp_rows),
            per_group(s_, 2 * LANES),
            per_group(s_ // SLC_CHUNK, V_ROWS, SLC_CHUNK),
            per_group(s_, LANES),
            per_group(s_ // WIN_CHUNK, V_ROWS, WIN_CHUNK),
        ],
        out_specs=pl.BlockSpec((1, TQ, HPG * HEAD_DIM), lambda b, g, t: (b, t, g)),
        out_shape=jax.ShapeDtypeStruct((b_, s_, N_HEADS * HEAD_DIM), F32),
        scratch_shapes=[pltpu.VMEM((2, 2 * LANES, HPG * TQ), BF16),
                        pltpu.VMEM((n_cmp_rows, HPG * TQ), F32),
                        pltpu.VMEM((TQ // LANES, n_cmp_rows, LANES), F32),
                        pltpu.VMEM((2, SLC_CHUNK, HPG * TQ), F32)],
        compiler_params=pltpu.CompilerParams(
            dimension_semantics=("arbitrary", "arbitrary", "arbitrary"), vmem_limit_bytes=VMEM_LIMIT),
        name="nsa_attn",
    )(qt, gt, kcmp, vct, ks, vst, kw, vwt)


def _nsa_heads(x, w_in, pe_k, w1_k, w2_k, pe_v, w1_v, w2_v):
    qt, gt, kc, vc, ks, vst, kw, vwt = _nsa_prep(x, w_in)
    kcmp, vct = _compress(kc, vc, pe_k, w1_k, w2_k, pe_v, w1_v, w2_v)
    return _nsa_attn(qt, gt, kcmp, vct, ks, vst, kw, vwt)


def kernel(x, l0_ffn1_w_in, l0_ffn1_w_out, l0_ln1_g, l0_ln1_b, l0_gm_w_in, l0_gm_ln_g, l0_gm_ln_b, l0_gm_w_s, l0_gm_b_s, l0_gm_w_out, l0_ln2_g, l0_ln2_b, l0_ffn2_w_in, l0_ffn2_w_out, l0_ln3_g, l0_ln3_b, l1_ffn1_w_in, l1_ffn1_w_out, l1_ln1_g, l1_ln1_b, l1_nsa_w_in, l1_nsa_cmp_pe_k, l1_nsa_cmp_w1_k, l1_nsa_cmp_w2_k, l1_nsa_cmp_pe_v, l1_nsa_cmp_w1_v, l1_nsa_cmp_w2_v, l1_nsa_w_out, l1_ln2_g, l1_ln2_b, l1_ffn2_w_in, l1_ffn2_w_out, l1_ln3_g, l1_ln3_b):
    b_, s_, d_ = x.shape
    h = x.reshape(b_ * s_, d_)
    h = _ffn_ln(h, l0_ffn1_w_in, l0_ffn1_w_out, l0_ln1_g, l0_ln1_b)
    h = _gmlp_ln(h, l0_gm_w_in, l0_gm_ln_g, l0_gm_ln_b, l0_gm_w_s, l0_gm_b_s, l0_gm_w_out, l0_ln2_g, l0_ln2_b)
    h = _ffn_ln(h, l0_ffn2_w_in, l0_ffn2_w_out, l0_ln3_g, l0_ln3_b)
    h = _ffn_ln(h, l1_ffn1_w_in, l1_ffn1_w_out, l1_ln1_g, l1_ln1_b)
    heads = _nsa_heads(h.reshape(b_, s_, d_), l1_nsa_w_in, l1_nsa_cmp_pe_k, l1_nsa_cmp_w1_k, l1_nsa_cmp_w2_k,
                       l1_nsa_cmp_pe_v, l1_nsa_cmp_w1_v, l1_nsa_cmp_w2_v)
    h = _ffn_ln(h, l1_ffn2_w_in, l1_ffn2_w_out, l1_ln3_g, l1_ln3_b,
                mix=(heads.reshape(b_ * s_, -1), l1_nsa_w_out, l1_ln2_g, l1_ln2_b))
    return h.reshape(b_, s_, d_)
```
